```python
import math
import jax
import jax.numpy as jnp
from jax import lax
import numpy as np

D_MODEL = 1024
BATCH = 4
SEQ = 4096
DEPTH = 2

EPS = 1e-6
CHUNK = 64
NEG_BIG = -1e30
TINY = 1e-30

HG_HEADS = 4
HG_DK = 128
HG_DV = 128
HG_KW = HG_HEADS * HG_DK
HG_VW = HG_HEADS * HG_DV

ML_HEADS = 4
ML_DK = 64
ML_DV = 128
ML_KW = ML_HEADS * ML_DK
ML_VW = ML_HEADS * ML_DV

AT_HEADS = 8
AT_KV_HEADS = 2
AT_GROUP = AT_HEADS // AT_KV_HEADS
AT_HD = 64
WINDOW = 128
N_BUCKETS = 32
MAX_DISTANCE = 128

DN_HEADS = 4
DN_DK = 128
DN_DV = 128
DN_KW = DN_HEADS * DN_DK
DN_VW = DN_HEADS * DN_DV
CONV_K = 4

N_BRANCH = 4
BRANCH_W = 512
D_FF = 4 * D_MODEL

IN_SPLITS = (
    HG_KW, HG_KW, HG_VW, HG_VW,
    ML_KW, ML_KW, ML_VW, ML_HEADS, ML_HEADS, ML_VW,
    AT_HEADS * AT_HD, AT_KV_HEADS * AT_HD, AT_KV_HEADS * AT_HD,
    2 * DN_KW + DN_VW, DN_HEADS, DN_HEADS, DN_VW,
    N_BRANCH * D_MODEL,
)
N_IN = sum(IN_SPLITS)

kernel_name = "hybrid_parallel_gated_mixers"


def _rmsnorm(x, g):
    xf = x.astype(jnp.float32)
    y = xf * lax.rsqrt(jnp.mean(xf * xf, axis=-1, keepdims=True) + EPS)
    return (y * g.astype(jnp.float32)).astype(x.dtype)


def _head_rms(x):
    return x * lax.rsqrt(jnp.mean(x * x, axis=-1, keepdims=True) + EPS)


def _l2norm(x):
    return x * lax.rsqrt(jnp.sum(x * x, axis=-1, keepdims=True) + EPS)


def _split_cols(z, sizes):
    parts, start = [], 0
    for s in sizes:
        parts.append(z[..., start:start + s])
        start += s
    return parts


def _chunk_heads(t, n_heads):
    b, t_len, w = t.shape
    return t.reshape(b, t_len // CHUNK, CHUNK, n_heads, w // n_heads).transpose(1, 0, 3, 2, 4)


def _chunk_gate(t):
    b, t_len, h = t.shape
    return t.reshape(b, t_len // CHUNK, CHUNK, h).transpose(1, 0, 3, 2)


def _unchunk(o):
    n, b, h, c, d = o.shape
    return o.transpose(1, 0, 3, 2, 4).reshape(b, n * c, h, d)


def _causal_dwconv(x, w):
    return lax.conv_general_dilated(
        x, w[:, None, :], window_strides=(1,), padding=((CONV_K - 1, 0),),
        dimension_numbers=('NWC', 'WIO', 'NWC'), feature_group_count=x.shape[-1])


def _t5_bucket(n):
    max_exact = N_BUCKETS // 2
    nf = jnp.maximum(n, max_exact).astype(jnp.float32)
    large = max_exact + (jnp.log(nf / max_exact) / math.log(MAX_DISTANCE / max_exact)
                         * (N_BUCKETS - max_exact)).astype(jnp.int32)
    large = jnp.minimum(large, N_BUCKETS - 1)
    return jnp.where(n < max_exact, n, large)


def hgrn2_mixer(q_pre, f_pre, i_pre, g_pre, lower_bound, out_g):
    f32 = jnp.float32
    b, t_len, _ = q_pre.shape
    z = f_pre.astype(f32)
    lb = lower_bound.astype(f32)
    f = lb + (1.0 - lb) * jax.nn.sigmoid(z)
    log_f = jnp.log(jnp.maximum(f, TINY))
    k = (1.0 - lb) * jax.nn.sigmoid(-z)
    q = jax.nn.silu(q_pre.astype(f32))
    qc = _chunk_heads(q, HG_HEADS)
    kc = _chunk_heads(k, HG_HEADS)
    lfc = _chunk_heads(log_f, HG_HEADS)
    vc = _chunk_heads(i_pre.astype(f32), HG_HEADS)
    causal = jnp.tril(jnp.ones((CHUNK, CHUNK), bool))

    def step(S, inp):
        q_, k_, v_, lf = inp
        cum = jnp.cumsum(lf, axis=2)
        diff = cum[:, :, :, None, :] - cum[:, :, None, :, :]
        dec = jnp.exp(jnp.where(causal[:, :, None], diff, NEG_BIG))
        a = jnp.einsum('bhtd,bhtsd,bhsd->bhts', q_, dec, k_)
        o = (jnp.einsum('bhts,bhsv->bhtv', a, v_)
             + jnp.einsum('bhtd,bhdv->bhtv', q_ * jnp.exp(cum), S))
        last = cum[:, :, -1:, :]
        S = (jnp.exp(last[:, :, 0, :])[..., None] * S
             + jnp.einsum('bhsd,bhsv->bhdv', k_ * jnp.exp(last - cum), v_))
        return S, o

    S0 = jnp.zeros((b, HG_HEADS, HG_DK, HG_DV), f32)
    _, o = lax.scan(step, S0, (qc, kc, vc, lfc))
    o = _head_rms(_unchunk(o)) * out_g.astype(f32).reshape(HG_HEADS, HG_DV)
    return o.reshape(b, t_len, HG_VW) * jax.nn.silu(g_pre.astype(f32))


def mlstm_mixer(q_pre, k_pre, v_pre, i_pre, f_pre, o_pre, if_bias, out_g):
    f32 = jnp.float32
    b, t_len, _ = q_pre.shape
    qc = _chunk_heads(q_pre.astype(f32), ML_HEADS)
    kc = _chunk_heads(k_pre.astype(f32), ML_HEADS) * (ML_DK ** -0.5)
    vc = _chunk_heads(v_pre.astype(f32), ML_HEADS)
    bias = if_bias.astype(f32)
    lic = _chunk_gate(i_pre.astype(f32) + bias[0])
    lfc = _chunk_gate(jax.nn.log_sigmoid(f_pre.astype(f32) + bias[1]))
    causal = jnp.tril(jnp.ones((CHUNK, CHUNK), bool))

    def step(carry, inp):
        Cm, n, m = carry
        q, k, v, li, lf = inp
        cum = jnp.cumsum(lf, axis=-1)
        logd = jnp.where(causal, cum[..., :, None] - cum[..., None, :] + li[..., None, :], NEG_BIG)
        m_inter = cum + m[..., None]
        m_t = jnp.maximum(m_inter, jnp.max(logd, axis=-1))
        s = jnp.einsum('bhtd,bhsd->bhts', q, k) * jnp.exp(logd - m_t[..., None])
        w_inter = jnp.exp(m_inter - m_t)
        num = (jnp.einsum('bhts,bhsv->bhtv', s, v)
               + w_inter[..., None] * jnp.einsum('bhtd,bhdv->bhtv', q, Cm))
        den = jnp.sum(s, axis=-1) + w_inter * jnp.einsum('bhtd,bhd->bht', q, n)
        h = num / jnp.maximum(jnp.abs(den), jnp.exp(-m_t))[..., None]
        m_new = m_t[..., -1]
        w_s = jnp.exp(cum[..., -1:] - cum + li - m_new[..., None])
        decay = jnp.exp(cum[..., -1] + m - m_new)
        Cm = decay[..., None, None] * Cm + jnp.einsum('bhs,bhsd,bhsv->bhdv', w_s, k, v)
        n = decay[..., None] * n + jnp.einsum('bhs,bhsd->bhd', w_s, k)
        return (Cm, n, m_new), h

    init = (jnp.zeros((b, ML_HEADS, ML_DK, ML_DV), f32),
            jnp.zeros((b, ML_HEADS, ML_DK), f32),
            jnp.zeros((b, ML_HEADS), f32))
    _, h = lax.scan(step, init, (qc, kc, vc, lic, lfc))
    h = _head_rms(_unchunk(h)) * out_g.astype(f32).reshape(ML_HEADS, ML_DV)
    return h.reshape(b, t_len, ML_VW) * jax.nn.sigmoid(o_pre.astype(f32))


def swa_mixer(q_pre, k_pre, v_pre, q_g, k_g, sinks, rel_table):
    f32 = jnp.float32
    b, t_len, _ = q_pre.shape
    nb = t_len // WINDOW
    q = _head_rms(q_pre.astype(f32).reshape(b, t_len, AT_HEADS, AT_HD)) * q_g.astype(f32)
    k = _head_rms(k_pre.astype(f32).reshape(b, t_len, AT_KV_HEADS, AT_HD)) * k_g.astype(f32)
    v = v_pre.astype(f32).reshape(b, t_len, AT_KV_HEADS, AT_HD)
    qb = q.reshape(b, nb, WINDOW, AT_KV_HEADS, AT_GROUP, AT_HD)
    kb = k.reshape(b, nb, WINDOW, AT_KV_HEADS, AT_HD)
    vb = v.reshape(b, nb, WINDOW, AT_KV_HEADS, AT_HD)
    pad = ((0, 0), (1, 0), (0, 0), (0, 0), (0, 0))
    kw = jnp.concatenate([jnp.pad(kb, pad)[:, :-1], kb], axis=2)
    vw = jnp.concatenate([jnp.pad(vb, pad)[:, :-1], vb], axis=2)
    logits = jnp.einsum('bnqkgd,bnskd->bnkgqs', qb, kw) * (AT_HD ** -0.5)
    qpos = jnp.arange(WINDOW)[:, None] + WINDOW
    kpos = jnp.arange(2 * WINDOW)[None, :]
    dist = qpos - kpos
    in_window = (dist >= 0) & (dist < WINDOW)
    bias = rel_table.astype(f32)[_t5_bucket(jnp.maximum(dist, 0))]
    bias = bias.transpose(2, 0, 1).reshape(AT_KV_HEADS, AT_GROUP, WINDOW, 2 * WINDOW)
    first = (jnp.arange(nb) == 0)[:, None, None]
    valid = in_window[None] & ~(first & (kpos < WINDOW)[None])
    logits = jnp.where(valid[None, :, None, None], logits + bias, NEG_BIG)
    sink = sinks.astype(f32).reshape(AT_KV_HEADS, AT_GROUP)[None, None, :, :, None, None]
    mx = jnp.maximum(jnp.max(logits, axis=-1, keepdims=True), sink)
    p = jnp.exp(logits - mx)
    denom = jnp.sum(p, axis=-1, keepdims=True) + jnp.exp(sink - mx)
    out = jnp.einsum('bnkgqs,bnskd->bnqkgd', p / denom, vw)
    return out.reshape(b, t_len, AT_HEADS * AT_HD)


def gated_deltanet_mixer(qkv_pre, beta_pre, a_pre, z_pre, conv_w, a_log, dt_bias, out_g):
    f32 = jnp.float32
    b, t_len, _ = qkv_pre.shape
    qkv = jax.nn.silu(_causal_dwconv(qkv_pre.astype(f32), conv_w.astype(f32)))
    q, k, v = _split_cols(qkv, (DN_KW, DN_KW, DN_VW))
    qc = _l2norm(_chunk_heads(q, DN_HEADS)) * (DN_DK ** -0.5)
    kc = _l2norm(_chunk_heads(k, DN_HEADS))
    vc = _chunk_heads(v, DN_HEADS)
    beta = _chunk_gate(jax.nn.sigmoid(beta_pre.astype(f32)))
    g = -jnp.exp(a_log.astype(f32)) * jax.nn.softplus(a_pre.astype(f32) + dt_bias.astype(f32))
    gam = jnp.cumsum(_chunk_gate(g), axis=-1)
    incl = jnp.tril(jnp.ones((CHUNK, CHUNK), bool))
    strict = jnp.tril(jnp.ones((CHUNK, CHUNK), bool), k=-1)
    decay = jnp.exp(jnp.where(incl, gam[..., :, None] - gam[..., None, :], NEG_BIG))
    kk = jnp.einsum('nbhtd,nbhsd->nbhts', kc, kc)
    lower = jnp.where(strict, beta[..., :, None] * kk * decay, 0.0) + jnp.eye(CHUNK, dtype=f32)
    rhs = jnp.concatenate([vc * beta[..., None], kc * (beta * jnp.exp(gam))[..., None]], axis=-1)
    sol = lax.linalg.triangular_solve(lower, rhs, left_side=True, lower=True, unit_diagonal=True)
    u, w = sol[..., :DN_DV], sol[..., DN_DV:]
    attn = jnp.einsum('nbhtd,nbhsd->nbhts', qc, kc) * decay
    q_dec = qc * jnp.exp(gam)[..., None]
    k_dec = kc * jnp.exp(gam[..., -1:] - gam)[..., None]
    g_last = jnp.exp(gam[..., -1])

    def step(S, inp):
        u_, w_, a_, qd, kd, gl = inp
        v_new = u_ - jnp.einsum('bhtk,bhkv->bhtv', w_, S)
        o = jnp.einsum('bhtk,bhkv->bhtv', qd, S) + jnp.einsum('bhts,bhsv->bhtv', a_, v_new)
        S = gl[..., None, None] * S + jnp.einsum('bhsk,bhsv->bhkv', kd, v_new)
        return S, o

    S0 = jnp.zeros((b, DN_HEADS, DN_DK, DN_DV), f32)
    _, o = lax.scan(step, S0, (u, w, attn, q_dec, k_dec, g_last))
    o = _head_rms(_unchunk(o)) * out_g.astype(f32)
    return o.reshape(b, t_len, DN_VW) * jax.nn.silu(z_pre.astype(f32))


def setup_inputs(seed: int = 0) -> dict:
    key = jax.random.key(seed)
    ks = jax.random.split(key, 24)

    def nrm(k, shape, scale):
        return jax.random.normal(k, shape, jnp.float32) * scale

    x = nrm(ks[0], (BATCH, SEQ, D_MODEL), 1.0)
    norm_mix_g = 1.0 + nrm(ks[1], (DEPTH, D_MODEL), 0.02)
    w_in = nrm(ks[2], (DEPTH, D_MODEL, N_IN), D_MODEL ** -0.5)
    hgrn_lb_table = nrm(ks[3], (DEPTH, HG_KW), 0.5)
    hgrn_out_g = 1.0 + nrm(ks[4], (DEPTH, HG_VW), 0.02)
    mlstm_if_bias = (jnp.array([-1.0, 3.0], jnp.float32)[None, :, None]
                     + nrm(ks[5], (DEPTH, 2, ML_HEADS), 0.3))
    mlstm_out_g = 1.0 + nrm(ks[6], (DEPTH, ML_VW), 0.02)
    attn_q_norm_g = 1.0 + nrm(ks[7], (DEPTH, AT_HD), 0.02)
    attn_k_norm_g = 1.0 + nrm(ks[8], (DEPTH, AT_HD), 0.02)
    attn_sinks = nrm(ks[9], (DEPTH, AT_HEADS), 0.5)
    rel_bias_table = nrm(ks[10], (N_BUCKETS, AT_HEADS), 0.5)
    dn_conv_w = nrm(ks[11], (DEPTH, CONV_K, 2 * DN_KW + DN_VW), CONV_K ** -0.5)
    dn_a_log = jnp.log(jax.random.uniform(ks[12], (DEPTH, DN_HEADS), jnp.float32, 1.0, 16.0))
    dt = jnp.exp(jax.random.uniform(ks[13], (DEPTH, DN_HEADS), jnp.float32,
                                    math.log(1e-3), math.log(1e-1)))
    dn_dt_bias = dt + jnp.log(-jnp.expm1(-dt))
    dn_out_g = 1.0 + nrm(ks[14], (DEPTH, DN_DV), 0.02)
    w_branch = nrm(ks[15], (DEPTH, N_BRANCH, BRANCH_W, D_MODEL), BRANCH_W ** -0.5)
    w_out = nrm(ks[16], (DEPTH, D_MODEL, D_MODEL), D_MODEL ** -0.5)
    norm_mlp_g = 1.0 + nrm(ks[17], (DEPTH, D_MODEL), 0.02)
    w_up = nrm(ks[18], (DEPTH, D_MODEL, D_FF), D_MODEL ** -0.5)
    w_down = nrm(ks[19], (DEPTH, D_FF, D_MODEL), D_FF ** -0.5)
    return {
        'x': x, 'norm_mix_g': norm_mix_g, 'w_in': w_in,
        'hgrn_lb_table': hgrn_lb_table, 'hgrn_out_g': hgrn_out_g,
        'mlstm_if_bias': mlstm_if_bias, 'mlstm_out_g': mlstm_out_g,
        'attn_q_norm_g': attn_q_norm_g, 'attn_k_norm_g': attn_k_norm_g,
        'attn_sinks': attn_sinks, 'rel_bias_table': rel_bias_table,
        'dn_conv_w': dn_conv_w, 'dn_a_log': dn_a_log, 'dn_dt_bias': dn_dt_bias,
        'dn_out_g': dn_out_g, 'w_branch': w_branch, 'w_out': w_out,
        'norm_mlp_g': norm_mlp_g, 'w_up': w_up, 'w_down': w_down,
    }


def reference(x, norm_mix_g, w_in, hgrn_lb_table, hgrn_out_g, mlstm_if_bias, mlstm_out_g,
              attn_q_norm_g, attn_k_norm_g, attn_sinks, rel_bias_table,
              dn_conv_w, dn_a_log, dn_dt_bias, dn_out_g,
              w_branch, w_out, norm_mlp_g, w_up, w_down):
    b, t_len, _ = x.shape
    lb_p = jax.nn.softmax(hgrn_lb_table.astype(jnp.float32), axis=0)
    lower_bounds = jnp.cumsum(lb_p, axis=0) - lb_p[0]
    for l in range(DEPTH):
        h = _rmsnorm(x, norm_mix_g[l])
        z = jnp.einsum('btd,dn->btn', h, w_in[l])
        (hq, hf, hi, hg, mq, mk, mv, mi, mf, mo, aq, ak, av,
         dqkv, db, da, dz, gate_pre) = _split_cols(z, IN_SPLITS)
        o_a = hgrn2_mixer(hq, hf, hi, hg, lower_bounds[l], hgrn_out_g[l])
        o_b = mlstm_mixer(mq, mk, mv, mi, mf, mo, mlstm_if_bias[l], mlstm_out_g[l])
        o_c = swa_mixer(aq, ak, av, attn_q_norm_g[l], attn_k_norm_g[l], attn_sinks[l], rel_bias_table)
        o_d = gated_deltanet_mixer(dqkv, db, da, dz, dn_conv_w[l], dn_a_log[l], dn_dt_bias[l], dn_out_g[l])
        branches = jnp.stack([o_a, o_b, o_c, o_d], axis=2).astype(x.dtype)
        proj = jnp.einsum('btnw,nwd->btnd', branches, w_branch[l])
        gates = jax.nn.sigmoid(gate_pre).reshape(b, t_len, N_BRANCH, D_MODEL)
        merged = jnp.sum(gates * proj, axis=2)
        x = x + merged @ w_out[l]
        h2 = _rmsnorm(x, norm_mlp_g[l])
        x = x + jnp.square(jax.nn.relu(h2 @ w_up[l])) @ w_down[l]
    return x
```

```python
import functools
import math

import numpy as np
import jax
import jax.numpy as jnp
from jax import lax
from jax.experimental import pallas as pl
from jax.experimental.pallas import tpu as pltpu

F32 = jnp.float32
BF16 = jnp.bfloat16

D_MODEL = 1024
EPS = 1e-6
CHUNK = 64
NEG_BIG = -1e30
TINY = 1e-30
LANES = 128
HEAD_W = 128
N_HEADS = 4
BRANCH_W = 512
N_BRANCH = 4
D_FF = 4 * D_MODEL

AT_HEADS = 8
AT_HD = 64
WINDOW = 128
N_BUCKETS = 32
MAX_DISTANCE = 128
CONV_K = 4
ML_DK = 64
DN_DK = 128
CHUNK_SHIFT = 6
SUB = 16
SUB_SHIFT = 4
AT_HD_SHIFT = 6
CONV_TAIL = 128

ZB_HQ, ZB_HI, ZB_HG = 0, 512, 1024
ZB_MQK, ZB_MV, ZB_MO = 1536, 2048, 2560
ZB_AQ, ZB_AKV = 3072, 3584
ZB_DQ, ZB_DK, ZB_DV, ZB_DZ = 4096, 4608, 5120, 5632
ZB_GATE = 6144
ZB_W = 10240
ZF_HF, ZF_S1, ZF_S2 = 0, 512, 640
ZF_W = 768

VMEM_LIMIT = 56 * 1024 * 1024


def _bf(x):
    return x.astype(BF16)


def _mm(a, b):
    return jnp.dot(_bf(a), _bf(b), preferred_element_type=F32)


def _mm_nt(a, b):
    return lax.dot_general(_bf(a), _bf(b), (((1,), (1,)), ((), ())),
                           preferred_element_type=F32)


def _mm_tn(a, b):
    return lax.dot_general(_bf(a), _bf(b), (((0,), (0,)), ((), ())),
                           preferred_element_type=F32)


def _split3(x):
    hi = _bf(x)
    r = x - hi.astype(F32)
    mid = _bf(r)
    lo = _bf(r - mid.astype(F32))
    return hi, mid, lo


def _mm3(sel, x):
    hi, mid, lo = _split3(x)
    return (jnp.dot(sel, hi, preferred_element_type=F32)
            + jnp.dot(sel, mid, preferred_element_type=F32)
            + jnp.dot(sel, lo, preferred_element_type=F32))


def _mm3_nt(sel, x):
    dn = (((1,), (1,)), ((), ()))
    hi, mid, lo = _split3(x)
    return (lax.dot_general(sel, hi, dn, preferred_element_type=F32)
            + lax.dot_general(sel, mid, dn, preferred_element_type=F32)
            + lax.dot_general(sel, lo, dn, preferred_element_type=F32))


def _sigmoid(x):
    return 1.0 / (1.0 + jnp.exp(-x))


def _silu(x):
    return x * _sigmoid(x)


def _log1pexp_neg_abs(x):
    return jnp.log(1.0 + jnp.exp(-jnp.abs(x)))


def _log_sigmoid(x):
    return jnp.minimum(x, 0.0) - _log1pexp_neg_abs(x)


def _softplus(x):
    return jnp.maximum(x, 0.0) + _log1pexp_neg_abs(x)


def _iota(shape, dim):
    return lax.broadcasted_iota(jnp.int32, shape, dim)


def _chunk_tril(n):
    t = _iota((n, n), 0)
    s = _iota((n, n), 1)
    return ((t >> CHUNK_SHIFT) == (s >> CHUNK_SHIFT)) & (s <= t)


def _head_rms(o):
    return o * lax.rsqrt(jnp.mean(o * o, axis=-1, keepdims=True) + EPS)


def _seg_ones(n, seg_shift):
    a = _iota((n, n), 0) >> seg_shift
    b = _iota((n, n), 1) >> seg_shift
    return jnp.where(a == b, 1.0, 0.0).astype(BF16)


def _inproj_kernel(x_ref, g_ref, wb_ref, wf_ref, zb_ref, zf_ref, h_ref):
    @pl.when(pl.program_id(1) == 0)
    def _():
        x = x_ref[...]
        ms = jnp.mean(x * x, axis=-1, keepdims=True)
        h = _bf((x * lax.rsqrt(ms + EPS)) * g_ref[...])
        h_ref[...] = h
        zf_ref[...] = jnp.dot(h, wf_ref[...], preferred_element_type=F32)

    zb_ref[...] = _bf(jnp.dot(h_ref[...], wb_ref[...], preferred_element_type=F32))


def _inproj(x2, g, wb, wf, tm, tn):
    n, d = x2.shape
    return pl.pallas_call(
        _inproj_kernel,
        grid=(n // tm, ZB_W // tn),
        in_specs=[
            pl.BlockSpec((tm, d), lambda i, j: (i, 0)),
            pl.BlockSpec((1, d), lambda i, j: (0, 0)),
            pl.BlockSpec((d, tn), lambda i, j: (0, j)),
            pl.BlockSpec((d, ZF_W), lambda i, j: (0, 0)),
        ],
        out_specs=[
            pl.BlockSpec((tm, tn), lambda i, j: (i, j)),
            pl.BlockSpec((tm, ZF_W), lambda i, j: (i, 0)),
        ],
        out_shape=[jax.ShapeDtypeStruct((n, ZB_W), BF16),
                   jax.ShapeDtypeStruct((n, ZF_W), F32)],
        scratch_shapes=[pltpu.VMEM((tm, d), BF16)],
        compiler_params=pltpu.CompilerParams(
            dimension_semantics=("arbitrary", "arbitrary"),
            vmem_limit_bytes=VMEM_LIMIT),
        name="inproj",
    )(x2, g, wb, wf)


def _hgrn_chunk(qc, kc, vc, cumc, st, consts):
    ones_b, sel_all, g_all, sub_iota, row_iota = consts
    n_sub = CHUNK // SUB

    pieces = []
    for blk in range(n_sub):
        r0 = blk * SUB
        qb = qc[r0:r0 + SUB]
        kb = kc[r0:r0 + SUB]
        cb = cumc[r0:r0 + SUB]
        for j in range(SUB):
            diff = cb[j:j + 1, :] - cb
            e = jnp.exp(jnp.where(sub_iota <= j, diff, NEG_BIG))
            pieces.append(_bf((qb[j:j + 1, :] * kb) * e))
    p_all = jnp.concatenate(pieces, axis=0)
    r_all = jnp.dot(p_all, ones_b, preferred_element_type=F32)
    a_diag = jnp.dot(g_all, _bf(r_all * sel_all), preferred_element_type=F32)

    strips = [jnp.zeros((SUB, CHUNK), F32)]
    for blk in range(1, n_sub):
        r0 = blk * SUB
        cr = cumc[r0 - 1:r0, :]
        qd = qc[r0:r0 + SUB] * jnp.exp(cumc[r0:r0 + SUB] - cr)
        kd = kc * jnp.exp(jnp.where(row_iota < r0, cr - cumc, NEG_BIG))
        strips.append(_mm_nt(qd, kd))
    a = a_diag[:, :CHUNK] + jnp.concatenate(strips, axis=0)

    last = cumc[CHUNK - 1:CHUNK, :]
    o = _mm(a, vc) + _mm_nt(qc * jnp.exp(cumc), st)
    ks = kc * jnp.exp(last - cumc)
    st_new = jnp.exp(last) * st + _mm_tn(vc, ks)
    return o, st_new


def _hgrn_kernel(q_ref, v_ref, g_ref, f_ref, lb_ref, og_ref, o_ref, st_ref, *, tb):
    @pl.when(pl.program_id(1) == 0)
    def _():
        st_ref[...] = jnp.zeros(st_ref.shape, F32)

    z = f_ref[...]
    lb = lb_ref[...]
    f = lb + (1.0 - lb) * _sigmoid(z)
    logf = jnp.log(jnp.maximum(f, TINY))
    k = (1.0 - lb) * _sigmoid(-z)
    q = _silu(q_ref[...].astype(F32))
    tri = jnp.where(_chunk_tril(tb), 1.0, 0.0).astype(BF16)
    cum = _mm3(tri, logf)

    n_rows = (CHUNK // SUB) * SUB * SUB
    r = _iota((n_rows, LANES), 0)
    c = _iota((n_rows, LANES), 1)
    sel_all = jnp.where(c == ((r >> (2 * SUB_SHIFT)) << SUB_SHIFT) + (r & (SUB - 1)), 1.0, 0.0)
    g_all = jnp.where((_iota((CHUNK, n_rows), 1) >> SUB_SHIFT) == _iota((CHUNK, n_rows), 0),
                      1.0, 0.0).astype(BF16)
    consts = (jnp.ones((LANES, LANES), BF16), sel_all, g_all,
              _iota((SUB, 1), 0), _iota((CHUNK, 1), 0))

    for h in range(N_HEADS):
        cs = slice(h * HEAD_W, (h + 1) * HEAD_W)
        st = st_ref[h]
        for ci in range(tb // CHUNK):
            rs = slice(ci * CHUNK, (ci + 1) * CHUNK)
            o, st = _hgrn_chunk(q[rs, cs], k[rs, cs], v_ref[rs, cs], cum[rs, cs], st, consts)
            o = _head_rms(o) * og_ref[:, cs] * _silu(g_ref[rs, cs].astype(F32))
            o_ref[rs, cs] = _bf(o)
        st_ref[h] = st


def _hgrn(zb, zf, lb, og, batch, seq, tb):
    nt = seq // tb
    row = lambda b, t: b * nt + t
    blk = lambda c: pl.BlockSpec((tb, BRANCH_W), lambda b, t, c=c: (row(b, t), c // BRANCH_W))
    vec = pl.BlockSpec((1, BRANCH_W), lambda b, t: (0, 0))
    return pl.pallas_call(
        functools.partial(_hgrn_kernel, tb=tb),
        grid=(batch, nt),
        in_specs=[blk(ZB_HQ), blk(ZB_HI), blk(ZB_HG), blk(ZF_HF), vec, vec],
        out_specs=pl.BlockSpec((tb, BRANCH_W), lambda b, t: (row(b, t), 0)),
        out_shape=jax.ShapeDtypeStruct((batch * seq, BRANCH_W), BF16),
        scratch_shapes=[pltpu.VMEM((N_HEADS, HEAD_W, HEAD_W), F32)],
        compiler_params=pltpu.CompilerParams(
            dimension_semantics=("arbitrary", "arbitrary"),
            vmem_limit_bytes=VMEM_LIMIT),
        name="hgrn2",
    )(zb, zb, zb, zf, lb, og)


def _mlstm_kernel(qk_ref, v_ref, og_pre_ref, s1_ref, s2_ref, b0_ref, b1_ref, gain_ref,
                  o_ref, c_ref, m_ref, *, tb):
    @pl.when(pl.program_id(1) == 0)
    def _():
        c_ref[...] = jnp.zeros(c_ref.shape, F32)
        m_ref[...] = jnp.zeros(m_ref.shape, F32)

    li_all = s1_ref[...] + b0_ref[...]
    lf_all = _log_sigmoid(s2_ref[...] + b1_ref[...])
    tri = jnp.where(_chunk_tril(tb), 1.0, 0.0).astype(BF16)
    cum_all = _mm3(tri, lf_all)
    rr_all = li_all - cum_all
    eye = jnp.where(_iota((16, LANES), 0) == _iota((16, LANES), 1), 1.0, 0.0).astype(BF16)
    causal = _iota((CHUNK, CHUNK), 1) <= _iota((CHUNK, CHUNK), 0)
    lane = _iota((1, LANES), 1)
    e0 = jnp.where(_iota((CHUNK, LANES), 1) == 0, 1.0, 0.0).astype(BF16)

    for ci in range(tb // CHUNK):
        rs = slice(ci * CHUNK, (ci + 1) * CHUNK)
        cum_c = cum_all[rs]
        li_c = li_all[rs]
        rr_t = _mm3_nt(eye, rr_all[rs])
        for h in range(N_HEADS):
            pair = h // 2
            half = (lane >> AT_HD_SHIFT) == (h % 2)
            gs = slice(pair * LANES, (pair + 1) * LANES)
            qm = jnp.where(half, qk_ref[rs, gs], jnp.zeros((), BF16))
            ks = qk_ref[rs, 2 * LANES + pair * LANES:2 * LANES + (pair + 1) * LANES].astype(F32) * (ML_DK ** -0.5)
            vs = slice(h * HEAD_W, (h + 1) * HEAD_W)
            v_aug = jnp.concatenate([v_ref[rs, vs], e0], axis=1)
            cum_col = cum_c[:, h:h + 1]
            li_col = li_c[:, h:h + 1]
            m_old = m_ref[h:h + 1, 0:1]
            cm = c_ref[h]

            logd = jnp.where(causal, cum_col + rr_t[h:h + 1, :], NEG_BIG)
            m_inter = cum_col + m_old
            m_t = jnp.maximum(m_inter, jnp.max(logd, axis=-1, keepdims=True))
            s_mat = _mm_nt(qm, ks) * jnp.exp(logd - m_t)
            w_inter = jnp.exp(m_inter - m_t)
            nd = _mm(s_mat, v_aug) + w_inter * _mm(qm, cm)
            num = nd[:, :HEAD_W]
            den = nd[:, HEAD_W:HEAD_W + 1]
            hval = num / jnp.maximum(jnp.abs(den), jnp.exp(-m_t))

            m_new = m_t[CHUNK - 1:CHUNK, :]
            cum_last = cum_col[CHUNK - 1:CHUNK, :]
            w_s = jnp.exp(cum_last - cum_col + li_col - m_new)
            decay = jnp.exp(cum_last + m_old - m_new)
            c_ref[h] = decay * cm + _mm_tn(ks * w_s, v_aug)
            m_ref[h:h + 1, :] = jnp.broadcast_to(m_new, (1, LANES))

            out = _head_rms(hval) * gain_ref[:, vs] * _sigmoid(og_pre_ref[rs, vs].astype(F32))
            o_ref[rs, vs] = _bf(out)


def _mlstm(zb, zf, b0, b1, gain, batch, seq, tb):
    nt = seq // tb
    row = lambda b, t: b * nt + t
    blk = lambda c: pl.BlockSpec((tb, BRANCH_W), lambda b, t, c=c: (row(b, t), c // BRANCH_W))
    sm = lambda c: pl.BlockSpec((tb, LANES), lambda b, t, c=c: (row(b, t), c // LANES))
    return pl.pallas_call(
        functools.partial(_mlstm_kernel, tb=tb),
        grid=(batch, nt),
        in_specs=[blk(ZB_MQK), blk(ZB_MV), blk(ZB_MO), sm(ZF_S1), sm(ZF_S2),
                  pl.BlockSpec((1, LANES), lambda b, t: (0, 0)),
                  pl.BlockSpec((1, LANES), lambda b, t: (0, 0)),
                  pl.BlockSpec((1, BRANCH_W), lambda b, t: (0, 0))],
        out_specs=pl.BlockSpec((tb, BRANCH_W), lambda b, t: (row(b, t), 0)),
        out_shape=jax.ShapeDtypeStruct((batch * seq, BRANCH_W), BF16),
        scratch_shapes=[pltpu.VMEM((N_HEADS, LANES, 2 * HEAD_W), F32),
                        pltpu.VMEM((8, LANES), F32)],
        compiler_params=pltpu.CompilerParams(
            dimension_semantics=("arbitrary", "arbitrary"),
            vmem_limit_bytes=VMEM_LIMIT),
        name="mlstm",
    )(zb, zb, zb, zf, zf, b0, b1, gain)


def _swa_kernel(q_ref, kvp_ref, kvc_ref, qg_ref, kg_ref, sink_ref, bias_ref, o_ref):
    first_cols = jnp.where(pl.program_id(1) == 0, WINDOW, 0)

    qf = q_ref[...].astype(F32)
    ms_q = _mm(qf * qf, _seg_ones(BRANCH_W, AT_HD_SHIFT)) * (1.0 / AT_HD)
    qn = _bf(qf * lax.rsqrt(ms_q + EPS) * qg_ref[...] * (AT_HD ** -0.5))

    kw = jnp.concatenate([kvp_ref[:, :LANES], kvc_ref[:, :LANES]], axis=0).astype(F32)
    vw = jnp.concatenate([kvp_ref[:, LANES:], kvc_ref[:, LANES:]], axis=0).astype(F32)
    ms_k = _mm(kw * kw, _seg_ones(LANES, AT_HD_SHIFT)) * (1.0 / AT_HD)
    kn = kw * lax.rsqrt(ms_k + EPS) * kg_ref[...]
    kn_sw = pltpu.roll(kn, AT_HD, 1)
    vw_sw = pltpu.roll(vw, AT_HD, 1)
    kn, kn_sw = _bf(kn), _bf(kn_sw)

    lane = _iota((1, LANES), 1)
    kpos = _iota((1, 2 * WINDOW), 1)
    for pair in range(AT_HEADS // 2):
        acc = jnp.zeros((WINDOW, LANES), F32)
        for half_i in range(2):
            h = 2 * pair + half_i
            kv = h // (AT_HEADS // 2)
            half = (lane >> AT_HD_SHIFT) == half_i
            qh = jnp.where(half, qn[:, pair * LANES:(pair + 1) * LANES], jnp.zeros((), BF16))
            k_h = kn if half_i == kv else kn_sw
            v_h = _bf(jnp.where(half, vw if half_i == kv else vw_sw, 0.0))
            lg = _mm_nt(qh, k_h) + bias_ref[h]
            lg = jnp.where(kpos < first_cols, NEG_BIG, lg)
            sink = sink_ref[h:h + 1, 0:1]
            mx = jnp.maximum(jnp.max(lg, axis=-1, keepdims=True), sink)
            p = jnp.exp(lg - mx)
            denom = jnp.sum(p, axis=-1, keepdims=True) + jnp.exp(sink - mx)
            acc = acc + _mm(p, v_h) / denom
        o_ref[:, pair * LANES:(pair + 1) * LANES] = _bf(acc)


def _swa(zb, qg, kg, sinks, bias, batch, seq):
    nt = seq // WINDOW
    row = lambda b, t: b * nt + t
    kvw = 2 * LANES
    return pl.pallas_call(
        _swa_kernel,
        grid=(batch, nt),
        in_specs=[
            pl.BlockSpec((WINDOW, BRANCH_W), lambda b, t: (row(b, t), ZB_AQ // BRANCH_W)),
            pl.BlockSpec((WINDOW, kvw), lambda b, t: (row(b, jnp.maximum(t - 1, 0)), ZB_AKV // kvw)),
            pl.BlockSpec((WINDOW, kvw), lambda b, t: (row(b, t), ZB_AKV // kvw)),
            pl.BlockSpec((1, BRANCH_W), lambda b, t: (0, 0)),
            pl.BlockSpec((1, LANES), lambda b, t: (0, 0)),
            pl.BlockSpec((AT_HEADS, LANES), lambda b, t: (0, 0)),
            pl.BlockSpec((AT_HEADS, WINDOW, 2 * WINDOW), lambda b, t: (0, 0, 0)),
        ],
        out_specs=pl.BlockSpec((WINDOW, BRANCH_W), lambda b, t: (row(b, t), 0)),
        out_shape=jax.ShapeDtypeStruct((batch * seq, BRANCH_W), BF16),
        compiler_params=pltpu.CompilerParams(
            dimension_semantics=("arbitrary", "arbitrary"),
            vmem_limit_bytes=VMEM_LIMIT),
        name="swa",
    )(zb, zb, zb, qg, kg, sinks, bias)


def _conv_silu(x_ref, prev_ref, w_ref, idx, shifts, has_prev):
    xb = x_ref[...]
    prev = prev_ref[...] * has_prev
    x2 = jnp.concatenate([prev, xb], axis=0)
    w = w_ref[:, idx * BRANCH_W:(idx + 1) * BRANCH_W]
    y = xb.astype(F32) * w[CONV_K - 1:CONV_K, :]
    for k in range(1, CONV_K):
        y = y + jnp.dot(shifts[k - 1], x2, preferred_element_type=F32) * w[CONV_K - 1 - k:CONV_K - k, :]
    return _silu(y)


def _dn_kernel(q_ref, k_ref, v_ref, qp_ref, kp_ref, vp_ref, z_ref, s1_ref, s2_ref, w_ref,
               alog_ref, dt_ref, gain_ref, o_ref, st_ref, *, tb):
    @pl.when(pl.program_id(1) == 0)
    def _():
        st_ref[...] = jnp.zeros(st_ref.shape, F32)

    n_chunks = tb // CHUNK
    t_i = _iota((tb, tb + CONV_TAIL), 0)
    r_i = _iota((tb, tb + CONV_TAIL), 1)
    shifts = [jnp.where(r_i == t_i + CONV_TAIL - k, 1.0, 0.0).astype(BF16) for k in range(1, CONV_K)]
    has_prev = jnp.where(pl.program_id(1) > 0, 1.0, 0.0).astype(BF16)
    q = _conv_silu(q_ref, qp_ref, w_ref, 0, shifts, has_prev)
    k = _conv_silu(k_ref, kp_ref, w_ref, 1, shifts, has_prev)
    v = _conv_silu(v_ref, vp_ref, w_ref, 2, shifts, has_prev)
    seg = _seg_ones(BRANCH_W, 7)
    q = q * lax.rsqrt(_mm(q * q, seg) + EPS) * (DN_DK ** -0.5)
    k = k * lax.rsqrt(_mm(k * k, seg) + EPS)

    beta_all = _sigmoid(s1_ref[...])
    g_all = -jnp.exp(alog_ref[...]) * _softplus(s2_ref[...] + dt_ref[...])
    incl = _chunk_tril(tb)
    tri = jnp.where(incl, 1.0, 0.0).astype(BF16)
    gam_all = _mm3(tri, g_all)
    eye = jnp.where(_iota((16, LANES), 0) == _iota((16, LANES), 1), 1.0, 0.0).astype(BF16)
    gam_t = _mm3_nt(eye, gam_all)
    t_sq = _iota((tb, tb), 0)
    s_sq = _iota((tb, tb), 1)
    level_masks = []
    for lv in range(CHUNK_SHIFT):
        m = 1 << lv
        level_masks.append(((t_sq >> (lv + 1)) == (s_sq >> (lv + 1)))
                           & ((t_sq & m) != 0) & ((s_sq & m) == 0))

    for h in range(N_HEADS):
        cs = slice(h * HEAD_W, (h + 1) * HEAD_W)
        gcol = gam_all[:, N_HEADS + h:N_HEADS + h + 1]
        bcol = beta_all[:, N_HEADS + h:N_HEADS + h + 1]
        grow = gam_t[N_HEADS + h:N_HEADS + h + 1, :]
        decay = jnp.exp(jnp.where(incl, gcol - grow, NEG_BIG))
        qh, kh, vh = q[:, cs], k[:, cs], v[:, cs]
        kb = _bf(kh)
        low = bcol * _mm_nt(kb, kb) * decay
        n_mat = -jnp.where(level_masks[0], low, 0.0)
        for lm in level_masks[1:]:
            l_m = jnp.where(lm, low, 0.0)
            x_m = l_m + _mm(l_m, n_mat)
            n_mat = n_mat - x_m - _mm(n_mat, x_m)
        rhs = jnp.concatenate([vh * bcol, kh * (bcol * jnp.exp(gcol))], axis=1)
        sol = rhs + _mm(n_mat, rhs)
        u, w = sol[:, :HEAD_W], sol[:, HEAD_W:]
        attn = _bf(_mm_nt(qh, kb) * decay)
        q_dec = qh * jnp.exp(gcol)

        st = st_ref[h]
        vnew = []
        for ci in range(n_chunks):
            rs = slice(ci * CHUNK, (ci + 1) * CHUNK)
            g_last = gcol[ci * CHUNK + CHUNK - 1:(ci + 1) * CHUNK, :]
            ws = _mm(jnp.concatenate([w[rs], q_dec[rs]], axis=0), st)
            v_new = u[rs] - ws[:CHUNK]
            vnew.append(_bf(v_new))
            pad = [jnp.zeros((CHUNK, HEAD_W), BF16)] * (n_chunks - ci - 1)
            v_full = jnp.concatenate(vnew + pad, axis=0)
            o = ws[CHUNK:] + jnp.dot(attn[rs], v_full, preferred_element_type=F32)
            k_dec = kh[rs] * jnp.exp(g_last - gcol[rs])
            st = jnp.exp(g_last) * st + _mm_tn(k_dec, v_new)
            out = _head_rms(o) * gain_ref[...] * _silu(z_ref[rs, cs].astype(F32))
            o_ref[rs, cs] = _bf(out)
        st_ref[h] = st


def _deltanet(zb, zf, conv_w, alog, dt, gain, batch, seq, tb):
    nt = seq // tb
    row = lambda b, t: b * nt + t
    blk = lambda c: pl.BlockSpec((tb, BRANCH_W), lambda b, t, c=c: (row(b, t), c // BRANCH_W))
    sm = lambda c: pl.BlockSpec((tb, LANES), lambda b, t, c=c: (row(b, t), c // LANES))
    vec = pl.BlockSpec((1, LANES), lambda b, t: (0, 0))
    per = tb // CONV_TAIL
    prev = lambda c: pl.BlockSpec(
        (CONV_TAIL, BRANCH_W),
        lambda b, t, c=c: (b * nt * per + jnp.maximum(t * per - 1, 0), c // BRANCH_W))
    return pl.pallas_call(
        functools.partial(_dn_kernel, tb=tb),
        grid=(batch, nt),
        in_specs=[blk(ZB_DQ), blk(ZB_DK), blk(ZB_DV), prev(ZB_DQ), prev(ZB_DK), prev(ZB_DV),
                  blk(ZB_DZ), sm(ZF_S1), sm(ZF_S2),
                  pl.BlockSpec((CONV_K, 3 * BRANCH_W), lambda b, t: (0, 0)), vec, vec, vec],
        out_specs=pl.BlockSpec((tb, BRANCH_W), lambda b, t: (row(b, t), 0)),
        out_shape=jax.ShapeDtypeStruct((batch * seq, BRANCH_W), BF16),
        scratch_shapes=[pltpu.VMEM((N_HEADS, DN_DK, HEAD_W), F32)],
        compiler_params=pltpu.CompilerParams(
            dimension_semantics=("arbitrary", "arbitrary"),
            vmem_limit_bytes=VMEM_LIMIT),
        name="deltanet",
    )(zb, zb, zb, zb, zb, zb, zb, zf, zf, conv_w, alog, dt, gain)


def _merge_kernel(oa_ref, ob_ref, oc_ref, od_ref, ga_ref, gb_ref, gc_ref, gd_ref,
                  x_ref, wb_ref, wo_ref, y_ref):
    merged = None
    branches = ((oa_ref, ga_ref), (ob_ref, gb_ref), (oc_ref, gc_ref), (od_ref, gd_ref))
    for n, (o_ref, gate_ref) in enumerate(branches):
        proj = jnp.dot(o_ref[...], wb_ref[n], preferred_element_type=F32)
        term = _sigmoid(gate_ref[...].astype(F32)) * proj
        merged = term if merged is None else merged + term
    y_ref[...] = x_ref[...] + jnp.dot(_bf(merged), wo_ref[...], preferred_element_type=F32)


def _merge(oa, ob, oc, od, zb, x2, wbr, wo, tm):
    n = x2.shape[0]
    ob_spec = pl.BlockSpec((tm, BRANCH_W), lambda i: (i, 0))
    gate = lambda k: pl.BlockSpec((tm, D_MODEL), lambda i, k=k: (i, ZB_GATE // D_MODEL + k))
    return pl.pallas_call(
        _merge_kernel,
        grid=(n // tm,),
        in_specs=[ob_spec, ob_spec, ob_spec, ob_spec,
                  gate(0), gate(1), gate(2), gate(3),
                  pl.BlockSpec((tm, D_MODEL), lambda i: (i, 0)),
                  pl.BlockSpec((N_BRANCH, BRANCH_W, D_MODEL), lambda i: (0, 0, 0)),
                  pl.BlockSpec((D_MODEL, D_MODEL), lambda i: (0, 0))],
        out_specs=pl.BlockSpec((tm, D_MODEL), lambda i: (i, 0)),
        out_shape=jax.ShapeDtypeStruct((n, D_MODEL), F32),
        compiler_params=pltpu.CompilerParams(
            dimension_semantics=("arbitrary",), vmem_limit_bytes=VMEM_LIMIT),
        name="merge",
    )(oa, ob, oc, od, zb, zb, zb, zb, x2, wbr, wo)


def _mlp_kernel(x_ref, g_ref, wu_ref, wd_ref, y_ref, h_ref, acc_ref):
    j = pl.program_id(1)

    @pl.when(j == 0)
    def _():
        x = x_ref[...]
        ms = jnp.mean(x * x, axis=-1, keepdims=True)
        h_ref[...] = _bf((x * lax.rsqrt(ms + EPS)) * g_ref[...])
        acc_ref[...] = jnp.zeros(acc_ref.shape, F32)

    up = jnp.dot(h_ref[...], wu_ref[...], preferred_element_type=F32)
    act = jnp.square(jnp.maximum(up, 0.0))
    acc_ref[...] += jnp.dot(_bf(act), wd_ref[...], preferred_element_type=F32)

    @pl.when(j == pl.num_programs(1) - 1)
    def _():
        y_ref[...] = x_ref[...] + acc_ref[...]


def _mlp(x2, g, wu, wd, tm, tf):
    n = x2.shape[0]
    return pl.pallas_call(
        _mlp_kernel,
        grid=(n // tm, D_FF // tf),
        in_specs=[pl.BlockSpec((tm, D_MODEL), lambda i, j: (i, 0)),
                  pl.BlockSpec((1, D_MODEL), lambda i, j: (0, 0)),
                  pl.BlockSpec((D_MODEL, tf), lambda i, j: (0, j)),
                  pl.BlockSpec((tf, D_MODEL), lambda i, j: (j, 0))],
        out_specs=pl.BlockSpec((tm, D_MODEL), lambda i, j: (i, 0)),
        out_shape=jax.ShapeDtypeStruct((n, D_MODEL), F32),
        scratch_shapes=[pltpu.VMEM((tm, D_MODEL), BF16), pltpu.VMEM((tm, D_MODEL), F32)],
        compiler_params=pltpu.CompilerParams(
            dimension_semantics=("arbitrary", "arbitrary"), vmem_limit_bytes=VMEM_LIMIT),
        name="mlp",
    )(x2, g, wu, wd)


def _t5_bucket_table():
    n = np.arange(WINDOW)
    max_exact = N_BUCKETS // 2
    nf = np.maximum(n, max_exact).astype(np.float32)
    large = max_exact + (np.log(nf / np.float32(max_exact)) / np.float32(math.log(MAX_DISTANCE / max_exact))
                         * (N_BUCKETS - max_exact)).astype(np.int32)
    large = np.minimum(large, N_BUCKETS - 1)
    return np.where(n < max_exact, n, large)


def _swa_bias(rel_table):
    qpos = np.arange(WINDOW)[:, None] + WINDOW
    kpos = np.arange(2 * WINDOW)[None, :]
    dist = qpos - kpos
    in_window = (dist >= 0) & (dist < WINDOW)
    bucket = _t5_bucket_table()[np.clip(dist, 0, WINDOW - 1)]
    bias = rel_table.astype(F32)[bucket]
    bias = jnp.where(in_window[:, :, None], bias, NEG_BIG)
    return bias.transpose(2, 0, 1)


def _lane_row(vals, offset):
    return jnp.zeros((1, LANES), F32).at[0, offset:offset + vals.shape[0]].set(vals.astype(F32))


def _layout_w_in(w):
    sizes = (512, 512, 512, 512, 256, 256, 512, 4, 4, 512, 512, 128, 128, 1536, 4, 4, 512, 4096)
    parts, start = [], 0
    for s in sizes:
        parts.append(w[:, start:start + s])
        start += s
    (hq, hf, hi, hg, mq, mk, mv, mi, mf, mo, aq, ak, av, dqkv, db, da, dz, gate) = parts
    zpad = lambda n: jnp.zeros((w.shape[0], n), w.dtype)
    wb = jnp.concatenate([hq, hi, hg, mq, mk, mv, mo, aq, ak, av, zpad(256), dqkv, dz, gate], axis=1)
    wf = jnp.concatenate([hf, mi, db, zpad(LANES - 8), mf, da, zpad(LANES - 8)], axis=1)
    return _bf(wb), _bf(wf)


def kernel(x, norm_mix_g, w_in, hgrn_lb_table, hgrn_out_g, mlstm_if_bias, mlstm_out_g,
           attn_q_norm_g, attn_k_norm_g, attn_sinks, rel_bias_table, dn_conv_w, dn_a_log,
           dn_dt_bias, dn_out_g, w_branch, w_out, norm_mlp_g, w_up, w_down):
    batch, seq, d = x.shape
    depth = w_in.shape[0]
    n = batch * seq
    tb = 256
    assert seq % tb == 0 and d == D_MODEL
    tm = 1024 if n % 1024 == 0 else 256

    lb_p = jax.nn.softmax(hgrn_lb_table.astype(F32), axis=0)
    lower_bounds = jnp.cumsum(lb_p, axis=0) - lb_p[0]
    bias = _swa_bias(rel_bias_table)

    x2 = x.reshape(n, d)
    for l in range(depth):
        wb, wf = _layout_w_in(w_in[l])
        zb, zf = _inproj(x2, norm_mix_g[l].reshape(1, d), wb, wf, tm, 2048)

        o_a = _hgrn(zb, zf, lower_bounds[l].reshape(1, -1), hgrn_out_g[l].reshape(1, -1).astype(F32),
                    batch, seq, tb)
        o_b = _mlstm(zb, zf, _lane_row(mlstm_if_bias[l, 0], 0), _lane_row(mlstm_if_bias[l, 1], 0),
                     mlstm_out_g[l].reshape(1, -1).astype(F32), batch, seq, tb)
        o_c = _swa(zb, jnp.tile(attn_q_norm_g[l].astype(F32), AT_HEADS).reshape(1, -1),
                   jnp.tile(attn_k_norm_g[l].astype(F32), 2).reshape(1, -1),
                   jnp.broadcast_to(attn_sinks[l].astype(F32)[:, None], (AT_HEADS, LANES)),
                   bias, batch, seq)
        o_d = _deltanet(zb, zf, dn_conv_w[l].astype(F32), _lane_row(dn_a_log[l], N_HEADS),
                        _lane_row(dn_dt_bias[l], N_HEADS), dn_out_g[l].reshape(1, -1).astype(F32),
                        batch, seq, tb)

        x2 = _merge(o_a, o_b, o_c, o_d, zb, x2, _bf(w_branch[l]), _bf(w_out[l]), 512 if n % 512 == 0 else 256)
        x2 = _mlp(x2, norm_mlp_g[l].reshape(1, d), _bf(w_up[l]), _bf(w_down[l]), tm, 1024)
    return x2.reshape(batch, seq, d)
```

```python
import functools
import math

import numpy as np
import jax
import jax.numpy as jnp
from jax import lax
from jax.experimental import pallas as pl
from jax.experimental.pallas import tpu as pltpu

F32 = jnp.float32
BF16 = jnp.bfloat16

D_MODEL = 1024
EPS = 1e-6
CHUNK = 64
NEG_BIG = -1e30
TINY = 1e-30
LANES = 128
HEAD_W = 128
N_HEADS = 4
BRANCH_W = 512
N_BRANCH = 4
D_FF = 4 * D_MODEL

AT_HEADS = 8
AT_HD = 64
WINDOW = 128
N_BUCKETS = 32
MAX_DISTANCE = 128
CONV_K = 4
ML_DK = 64
DN_DK = 128
CHUNK_SHIFT = 6
SUB = 16
SUB_SHIFT = 4
AT_HD_SHIFT = 6
CONV_TAIL = 128

ZB_HQ, ZB_HI, ZB_HG = 0, 512, 1024
ZB_MQK, ZB_MV, ZB_MO = 1536, 2048, 2560
ZB_AQ, ZB_AKV = 3072, 3584
ZB_DQ, ZB_DK, ZB_DV, ZB_DZ = 4096, 4608, 5120, 5632
ZB_GATE = 6144
ZB_W = 10240
ZF_HF, ZF_S1, ZF_S2 = 0, 512, 640
ZF_W = 768

VMEM_LIMIT = 56 * 1024 * 1024


def _bf(x):
    return x.astype(BF16)


def _mm(a, b):
    return jnp.dot(_bf(a), _bf(b), preferred_element_type=F32)


def _mm_nt(a, b):
    return lax.dot_general(_bf(a), _bf(b), (((1,), (1,)), ((), ())),
                           preferred_element_type=F32)


def _mm_tn(a, b):
    return lax.dot_general(_bf(a), _bf(b), (((0,), (0,)), ((), ())),
                           preferred_element_type=F32)


def _split3(x):
    hi = _bf(x)
    r = x - hi.astype(F32)
    mid = _bf(r)
    lo = _bf(r - mid.astype(F32))
    return hi, mid, lo


def _mm3(sel, x):
    hi, mid, lo = _split3(x)
    return (jnp.dot(sel, hi, preferred_element_type=F32)
            + jnp.dot(sel, mid, preferred_element_type=F32)
            + jnp.dot(sel, lo, preferred_element_type=F32))


def _mm3_nt(sel, x):
    dn = (((1,), (1,)), ((), ()))
    hi, mid, lo = _split3(x)
    return (lax.dot_general(sel, hi, dn, preferred_element_type=F32)
            + lax.dot_general(sel, mid, dn, preferred_element_type=F32)
            + lax.dot_general(sel, lo, dn, preferred_element_type=F32))


def _sigmoid(x):
    return 1.0 / (1.0 + jnp.exp(-x))


def _silu(x):
    return x * _sigmoid(x)


def _log1pexp_neg_abs(x):
    return jnp.log(1.0 + jnp.exp(-jnp.abs(x)))


def _log_sigmoid(x):
    return jnp.minimum(x, 0.0) - _log1pexp_neg_abs(x)


def _softplus(x):
    return jnp.maximum(x, 0.0) + _log1pexp_neg_abs(x)


def _iota(shape, dim):
    return lax.broadcasted_iota(jnp.int32, shape, dim)


def _chunk_tril(n):
    t = _iota((n, n), 0)
    s = _iota((n, n), 1)
    return ((t >> CHUNK_SHIFT) == (s >> CHUNK_SHIFT)) & (s <= t)


def _head_rms(o):
    return o * lax.rsqrt(jnp.mean(o * o, axis=-1, keepdims=True) + EPS)


def _seg_ones(n, seg_shift):
    a = _iota((n, n), 0) >> seg_shift
    b = _iota((n, n), 1) >> seg_shift
    return jnp.where(a == b, 1.0, 0.0).astype(BF16)


def _inproj_kernel(x_ref, g_ref, wb_ref, wf_ref, zb_ref, zf_ref, h_ref):
    @pl.when(pl.program_id(1) == 0)
    def _():
        x = x_ref[...]
        ms = jnp.mean(x * x, axis=-1, keepdims=True)
        h = _bf((x * lax.rsqrt(ms + EPS)) * g_ref[...])
        h_ref[...] = h
        zf_ref[...] = jnp.dot(h, wf_ref[...], preferred_element_type=F32)

    zb_ref[...] = _bf(jnp.dot(h_ref[...], wb_ref[...], preferred_element_type=F32))


def _inproj(x2, g, wb, wf, tm, tn):
    n, d = x2.shape
    return pl.pallas_call(
        _inproj_kernel,
        grid=(n // tm, ZB_W // tn),
        in_specs=[
            pl.BlockSpec((tm, d), lambda i, j: (i, 0)),
            pl.BlockSpec((1, d), lambda i, j: (0, 0)),
            pl.BlockSpec((d, tn), lambda i, j: (0, j)),
            pl.BlockSpec((d, ZF_W), lambda i, j: (0, 0)),
        ],
        out_specs=[
            pl.BlockSpec((tm, tn), lambda i, j: (i, j)),
            pl.BlockSpec((tm, ZF_W), lambda i, j: (i, 0)),
        ],
        out_shape=[jax.ShapeDtypeStruct((n, ZB_W), BF16),
                   jax.ShapeDtypeStruct((n, ZF_W), F32)],
        scratch_shapes=[pltpu.VMEM((tm, d), BF16)],
        compiler_params=pltpu.CompilerParams(
            dimension_semantics=("arbitrary", "arbitrary"),
            vmem_limit_bytes=VMEM_LIMIT),
        name="inproj",
    )(x2, g, wb, wf)


def _hgrn_chunk(qc, kc, vc, cumc, st, consts):
    ones_b, sel_all, g_all, sub_iota, row_iota = consts
    n_sub = CHUNK // SUB

    pieces = []
    for blk in range(n_sub):
        r0 = blk * SUB
        qb = qc[r0:r0 + SUB]
        kb = kc[r0:r0 + SUB]
        cb = cumc[r0:r0 + SUB]
        for j in range(SUB):
            diff = cb[j:j + 1, :] - cb
            e = jnp.exp(jnp.where(sub_iota <= j, diff, NEG_BIG))
            pieces.append(_bf((qb[j:j + 1, :] * kb) * e))
    p_all = jnp.concatenate(pieces, axis=0)
    r_all = jnp.dot(p_all, ones_b, preferred_element_type=F32)
    a_diag = jnp.dot(g_all, _bf(r_all * sel_all), preferred_element_type=F32)

    strips = [jnp.zeros((SUB, CHUNK), F32)]
    for blk in range(1, n_sub):
        r0 = blk * SUB
        cr = cumc[r0 - 1:r0, :]
        qd = qc[r0:r0 + SUB] * jnp.exp(cumc[r0:r0 + SUB] - cr)
        kd = kc * jnp.exp(jnp.where(row_iota < r0, cr - cumc, NEG_BIG))
        strips.append(_mm_nt(qd, kd))
    a = a_diag[:, :CHUNK] + jnp.concatenate(strips, axis=0)

    last = cumc[CHUNK - 1:CHUNK, :]
    o = _mm(a, vc) + _mm_nt(qc * jnp.exp(cumc), st)
    ks = kc * jnp.exp(last - cumc)
    st_new = jnp.exp(last) * st + _mm_tn(vc, ks)
    return o, st_new


def _hgrn_kernel(q_ref, v_ref, g_ref, f_ref, lb_ref, og_ref, o_ref, st_ref, *, tb):
    @pl.when(pl.program_id(1) == 0)
    def _():
        st_ref[...] = jnp.zeros(st_ref.shape, F32)

    z = f_ref[...]
    lb = lb_ref[...]
    f = lb + (1.0 - lb) * _sigmoid(z)
    logf = jnp.log(jnp.maximum(f, TINY))
    k = (1.0 - lb) * _sigmoid(-z)
    q = _silu(q_ref[...].astype(F32))
    tri = jnp.where(_chunk_tril(tb), 1.0, 0.0).astype(BF16)
    cum = _mm3(tri, logf)

    n_rows = (CHUNK // SUB) * SUB * SUB
    r = _iota((n_rows, LANES), 0)
    c = _iota((n_rows, LANES), 1)
    sel_all = jnp.where(c == ((r >> (2 * SUB_SHIFT)) << SUB_SHIFT) + (r & (SUB - 1)), 1.0, 0.0)
    g_all = jnp.where((_iota((CHUNK, n_rows), 1) >> SUB_SHIFT) == _iota((CHUNK, n_rows), 0),
                      1.0, 0.0).astype(BF16)
    consts = (jnp.ones((LANES, LANES), BF16), sel_all, g_all,
              _iota((SUB, 1), 0), _iota((CHUNK, 1), 0))

    st = [st_ref[h] for h in range(N_HEADS)]
    for ci in range(tb // CHUNK):
        rs = slice(ci * CHUNK, (ci + 1) * CHUNK)
        for h in range(N_HEADS):
            cs = slice(h * HEAD_W, (h + 1) * HEAD_W)
            o, st[h] = _hgrn_chunk(q[rs, cs], k[rs, cs], v_ref[rs, cs], cum[rs, cs], st[h], consts)
            o = _head_rms(o) * og_ref[:, cs] * _silu(g_ref[rs, cs].astype(F32))
            o_ref[rs, cs] = _bf(o)
    for h in range(N_HEADS):
        st_ref[h] = st[h]


def _hgrn(zb, zf, lb, og, batch, seq, tb):
    nt = seq // tb
    row = lambda b, t: b * nt + t
    blk = lambda c: pl.BlockSpec((tb, BRANCH_W), lambda b, t, c=c: (row(b, t), c // BRANCH_W))
    vec = pl.BlockSpec((1, BRANCH_W), lambda b, t: (0, 0))
    return pl.pallas_call(
        functools.partial(_hgrn_kernel, tb=tb),
        grid=(batch, nt),
        in_specs=[blk(ZB_HQ), blk(ZB_HI), blk(ZB_HG), blk(ZF_HF), vec, vec],
        out_specs=pl.BlockSpec((tb, BRANCH_W), lambda b, t: (row(b, t), 0)),
        out_shape=jax.ShapeDtypeStruct((batch * seq, BRANCH_W), BF16),
        scratch_shapes=[pltpu.VMEM((N_HEADS, HEAD_W, HEAD_W), F32)],
        compiler_params=pltpu.CompilerParams(
            dimension_semantics=("arbitrary", "arbitrary"),
            vmem_limit_bytes=VMEM_LIMIT),
        name="hgrn2",
    )(zb, zb, zb, zf, lb, og)


def _mlstm_kernel(qk_ref, v_ref, og_pre_ref, s1_ref, s2_ref, b0_ref, b1_ref, gain_ref,
                  o_ref, c_ref, m_ref, *, tb):
    @pl.when(pl.program_id(1) == 0)
    def _():
        c_ref[...] = jnp.zeros(c_ref.shape, F32)
        m_ref[...] = jnp.zeros(m_ref.shape, F32)

    li_all = s1_ref[...] + b0_ref[...]
    lf_all = _log_sigmoid(s2_ref[...] + b1_ref[...])
    tri = jnp.where(_chunk_tril(tb), 1.0, 0.0).astype(BF16)
    cum_all = _mm3(tri, lf_all)
    rr_all = li_all - cum_all
    eye = jnp.where(_iota((16, LANES), 0) == _iota((16, LANES), 1), 1.0, 0.0).astype(BF16)
    causal = _iota((CHUNK, CHUNK), 1) <= _iota((CHUNK, CHUNK), 0)
    lane = _iota((1, LANES), 1)
    e0 = jnp.where(_iota((CHUNK, LANES), 1) == 0, 1.0, 0.0).astype(BF16)

    cms = [c_ref[h] for h in range(N_HEADS)]
    ms = [m_ref[h:h + 1, 0:1] for h in range(N_HEADS)]
    for ci in range(tb // CHUNK):
        rs = slice(ci * CHUNK, (ci + 1) * CHUNK)
        cum_c = cum_all[rs]
        li_c = li_all[rs]
        rr_t = _mm3_nt(eye, rr_all[rs])
        for h in range(N_HEADS):
            pair = h // 2
            half = (lane >> AT_HD_SHIFT) == (h % 2)
            gs = slice(pair * LANES, (pair + 1) * LANES)
            qm = jnp.where(half, qk_ref[rs, gs], jnp.zeros((), BF16))
            ks = qk_ref[rs, 2 * LANES + pair * LANES:2 * LANES + (pair + 1) * LANES].astype(F32) * (ML_DK ** -0.5)
            vs = slice(h * HEAD_W, (h + 1) * HEAD_W)
            v_aug = jnp.concatenate([v_ref[rs, vs], e0], axis=1)
            cum_col = cum_c[:, h:h + 1]
            li_col = li_c[:, h:h + 1]
            m_old = ms[h]
            cm = cms[h]

            logd = jnp.where(causal, cum_col + rr_t[h:h + 1, :], NEG_BIG)
            m_inter = cum_col + m_old
            m_t = jnp.maximum(m_inter, jnp.max(logd, axis=-1, keepdims=True))
            s_mat = _mm_nt(qm, ks) * jnp.exp(logd - m_t)
            w_inter = jnp.exp(m_inter - m_t)
            nd = _mm(s_mat, v_aug) + w_inter * _mm(qm, cm)
            num = nd[:, :HEAD_W]
            den = nd[:, HEAD_W:HEAD_W + 1]
            hval = num / jnp.maximum(jnp.abs(den), jnp.exp(-m_t))

            m_new = m_t[CHUNK - 1:CHUNK, :]
            cum_last = cum_col[CHUNK - 1:CHUNK, :]
            w_s = jnp.exp(cum_last - cum_col + li_col - m_new)
            decay = jnp.exp(cum_last + m_old - m_new)
            cms[h] = decay * cm + _mm_tn(ks * w_s, v_aug)
            ms[h] = m_new

            out = _head_rms(hval) * gain_ref[:, vs] * _sigmoid(og_pre_ref[rs, vs].astype(F32))
            o_ref[rs, vs] = _bf(out)
    for h in range(N_HEADS):
        c_ref[h] = cms[h]
        m_ref[h:h + 1, :] = jnp.broadcast_to(ms[h], (1, LANES))


def _mlstm(zb, zf, b0, b1, gain, batch, seq, tb):
    nt = seq // tb
    row = lambda b, t: b * nt + t
    blk = lambda c: pl.BlockSpec((tb, BRANCH_W), lambda b, t, c=c: (row(b, t), c // BRANCH_W))
    sm = lambda c: pl.BlockSpec((tb, LANES), lambda b, t, c=c: (row(b, t), c // LANES))
    return pl.pallas_call(
        functools.partial(_mlstm_kernel, tb=tb),
        grid=(batch, nt),
        in_specs=[blk(ZB_MQK), blk(ZB_MV), blk(ZB_MO), sm(ZF_S1), sm(ZF_S2),
                  pl.BlockSpec((1, LANES), lambda b, t: (0, 0)),
                  pl.BlockSpec((1, LANES), lambda b, t: (0, 0)),
                  pl.BlockSpec((1, BRANCH_W), lambda b, t: (0, 0))],
        out_specs=pl.BlockSpec((tb, BRANCH_W), lambda b, t: (row(b, t), 0)),
        out_shape=jax.ShapeDtypeStruct((batch * seq, BRANCH_W), BF16),
        scratch_shapes=[pltpu.VMEM((N_HEADS, LANES, 2 * HEAD_W), F32),
                        pltpu.VMEM((8, LANES), F32)],
        compiler_params=pltpu.CompilerParams(
            dimension_semantics=("arbitrary", "arbitrary"),
            vmem_limit_bytes=VMEM_LIMIT),
        name="mlstm",
    )(zb, zb, zb, zf, zf, b0, b1, gain)


def _swa_kernel(q_ref, kvp_ref, kvc_ref, qg_ref, kg_ref, sink_ref, bias_ref, o_ref):
    first_cols = jnp.where(pl.program_id(1) == 0, WINDOW, 0)

    qf = q_ref[...].astype(F32)
    ms_q = _mm(qf * qf, _seg_ones(BRANCH_W, AT_HD_SHIFT)) * (1.0 / AT_HD)
    qn = _bf(qf * lax.rsqrt(ms_q + EPS) * qg_ref[...] * (AT_HD ** -0.5))

    kw = jnp.concatenate([kvp_ref[:, :LANES], kvc_ref[:, :LANES]], axis=0).astype(F32)
    vw = jnp.concatenate([kvp_ref[:, LANES:], kvc_ref[:, LANES:]], axis=0).astype(F32)
    ms_k = _mm(kw * kw, _seg_ones(LANES, AT_HD_SHIFT)) * (1.0 / AT_HD)
    kn = kw * lax.rsqrt(ms_k + EPS) * kg_ref[...]
    kn_sw = pltpu.roll(kn, AT_HD, 1)
    vw_sw = pltpu.roll(vw, AT_HD, 1)
    kn, kn_sw = _bf(kn), _bf(kn_sw)

    lane = _iota((1, LANES), 1)
    kpos = _iota((1, 2 * WINDOW), 1)
    for pair in range(AT_HEADS // 2):
        acc = jnp.zeros((WINDOW, LANES), F32)
        for half_i in range(2):
            h = 2 * pair + half_i
            kv = h // (AT_HEADS // 2)
            half = (lane >> AT_HD_SHIFT) == half_i
            qh = jnp.where(half, qn[:, pair * LANES:(pair + 1) * LANES], jnp.zeros((), BF16))
            k_h = kn if half_i == kv else kn_sw
            v_h = _bf(jnp.where(half, vw if half_i == kv else vw_sw, 0.0))
            lg = _mm_nt(qh, k_h) + bias_ref[h]
            lg = jnp.where(kpos < first_cols, NEG_BIG, lg)
            sink = sink_ref[h:h + 1, 0:1]
            mx = jnp.maximum(jnp.max(lg, axis=-1, keepdims=True), sink)
            p = jnp.exp(lg - mx)
            denom = jnp.sum(p, axis=-1, keepdims=True) + jnp.exp(sink - mx)
            acc = acc + _mm(p, v_h) / denom
        o_ref[:, pair * LANES:(pair + 1) * LANES] = _bf(acc)


def _swa(zb, qg, kg, sinks, bias, batch, seq):
    nt = seq // WINDOW
    row = lambda b, t: b * nt + t
    kvw = 2 * LANES
    return pl.pallas_call(
        _swa_kernel,
        grid=(batch, nt),
        in_specs=[
            pl.BlockSpec((WINDOW, BRANCH_W), lambda b, t: (row(b, t), ZB_AQ // BRANCH_W)),
            pl.BlockSpec((WINDOW, kvw), lambda b, t: (row(b, jnp.maximum(t - 1, 0)), ZB_AKV // kvw)),
            pl.BlockSpec((WINDOW, kvw), lambda b, t: (row(b, t), ZB_AKV // kvw)),
            pl.BlockSpec((1, BRANCH_W), lambda b, t: (0, 0)),
            pl.BlockSpec((1, LANES), lambda b, t: (0, 0)),
            pl.BlockSpec((AT_HEADS, LANES), lambda b, t: (0, 0)),
            pl.BlockSpec((AT_HEADS, WINDOW, 2 * WINDOW), lambda b, t: (0, 0, 0)),
        ],
        out_specs=pl.BlockSpec((WINDOW, BRANCH_W), lambda b, t: (row(b, t), 0)),
        out_shape=jax.ShapeDtypeStruct((batch * seq, BRANCH_W), BF16),
        compiler_params=pltpu.CompilerParams(
            dimension_semantics=("arbitrary", "arbitrary"),
            vmem_limit_bytes=VMEM_LIMIT),
        name="swa",
    )(zb, zb, zb, qg, kg, sinks, bias)


def _conv_silu(x_ref, prev_ref, w_ref, idx, shifts, has_prev):
    xb = x_ref[...]
    prev = prev_ref[...] * has_prev
    x2 = jnp.concatenate([prev, xb], axis=0)
    w = w_ref[:, idx * BRANCH_W:(idx + 1) * BRANCH_W]
    y = xb.astype(F32) * w[CONV_K - 1:CONV_K, :]
    for k in range(1, CONV_K):
        y = y + jnp.dot(shifts[k - 1], x2, preferred_element_type=F32) * w[CONV_K - 1 - k:CONV_K - k, :]
    return _silu(y)


def _dn_kernel(q_ref, k_ref, v_ref, qp_ref, kp_ref, vp_ref, z_ref, s1_ref, s2_ref, w_ref,
               alog_ref, dt_ref, gain_ref, o_ref, st_ref, *, tb):
    @pl.when(pl.program_id(1) == 0)
    def _():
        st_ref[...] = jnp.zeros(st_ref.shape, F32)

    n_chunks = tb // CHUNK
    t_i = _iota((tb, tb + CONV_TAIL), 0)
    r_i = _iota((tb, tb + CONV_TAIL), 1)
    shifts = [jnp.where(r_i == t_i + CONV_TAIL - k, 1.0, 0.0).astype(BF16) for k in range(1, CONV_K)]
    has_prev = jnp.where(pl.program_id(1) > 0, 1.0, 0.0).astype(BF16)
    q = _conv_silu(q_ref, qp_ref, w_ref, 0, shifts, has_prev)
    k = _conv_silu(k_ref, kp_ref, w_ref, 1, shifts, has_prev)
    v = _conv_silu(v_ref, vp_ref, w_ref, 2, shifts, has_prev)
    seg = _seg_ones(BRANCH_W, 7)
    q = q * lax.rsqrt(_mm(q * q, seg) + EPS) * (DN_DK ** -0.5)
    k = k * lax.rsqrt(_mm(k * k, seg) + EPS)

    beta_all = _sigmoid(s1_ref[...])
    g_all = -jnp.exp(alog_ref[...]) * _softplus(s2_ref[...] + dt_ref[...])
    incl = _chunk_tril(tb)
    tri = jnp.where(incl, 1.0, 0.0).astype(BF16)
    gam_all = _mm3(tri, g_all)
    eye = jnp.where(_iota((16, LANES), 0) == _iota((16, LANES), 1), 1.0, 0.0).astype(BF16)
    gam_t = _mm3_nt(eye, gam_all)
    t_sq = _iota((tb, tb), 0)
    s_sq = _iota((tb, tb), 1)
    level_masks = []
    for lv in range(CHUNK_SHIFT):
        m = 1 << lv
        level_masks.append(((t_sq >> (lv + 1)) == (s_sq >> (lv + 1)))
                           & ((t_sq & m) != 0) & ((s_sq & m) == 0))

    heads = range(N_HEADS)
    cs = [slice(h * HEAD_W, (h + 1) * HEAD_W) for h in heads]
    gcol = [gam_all[:, N_HEADS + h:N_HEADS + h + 1] for h in heads]
    bcol = [beta_all[:, N_HEADS + h:N_HEADS + h + 1] for h in heads]
    decay = [jnp.exp(jnp.where(incl, gcol[h] - gam_t[N_HEADS + h:N_HEADS + h + 1, :], NEG_BIG))
             for h in heads]
    kb = [_bf(k[:, cs[h]]) for h in heads]
    low = [bcol[h] * _mm_nt(kb[h], kb[h]) * decay[h] for h in heads]
    n_mat = [-jnp.where(level_masks[0], low[h], 0.0) for h in heads]
    for lm in level_masks[1:]:
        l_m = [jnp.where(lm, low[h], 0.0) for h in heads]
        x_m = [l_m[h] + _mm(l_m[h], n_mat[h]) for h in heads]
        n_mat = [n_mat[h] - x_m[h] - _mm(n_mat[h], x_m[h]) for h in heads]
    rhs = [jnp.concatenate([v[:, cs[h]] * bcol[h], k[:, cs[h]] * (bcol[h] * jnp.exp(gcol[h]))], axis=1)
           for h in heads]
    sol = [rhs[h] + _mm(n_mat[h], rhs[h]) for h in heads]
    attn = [_bf(_mm_nt(q[:, cs[h]], kb[h]) * decay[h]) for h in heads]
    q_dec = [q[:, cs[h]] * jnp.exp(gcol[h]) for h in heads]

    st = [st_ref[h] for h in heads]
    vnew = [[] for _ in heads]
    for ci in range(n_chunks):
        rs = slice(ci * CHUNK, (ci + 1) * CHUNK)
        pad = [jnp.zeros((CHUNK, HEAD_W), BF16)] * (n_chunks - ci - 1)
        for h in heads:
            g_last = gcol[h][ci * CHUNK + CHUNK - 1:(ci + 1) * CHUNK, :]
            ws = _mm(jnp.concatenate([sol[h][rs, HEAD_W:], q_dec[h][rs]], axis=0), st[h])
            v_new = sol[h][rs, :HEAD_W] - ws[:CHUNK]
            vnew[h].append(_bf(v_new))
            v_full = jnp.concatenate(vnew[h] + pad, axis=0)
            o = ws[CHUNK:] + jnp.dot(attn[h][rs], v_full, preferred_element_type=F32)
            k_dec = k[rs, cs[h]] * jnp.exp(g_last - gcol[h][rs])
            st[h] = jnp.exp(g_last) * st[h] + _mm_tn(k_dec, v_new)
            out = _head_rms(o) * gain_ref[...] * _silu(z_ref[rs, cs[h]].astype(F32))
            o_ref[rs, cs[h]] = _bf(out)
    for h in heads:
        st_ref[h] = st[h]


def _deltanet(zb, zf, conv_w, alog, dt, gain, batch, seq, tb):
    nt = seq // tb
    row = lambda b, t: b * nt + t
    blk = lambda c: pl.BlockSpec((tb, BRANCH_W), lambda b, t, c=c: (row(b, t), c // BRANCH_W))
    sm = lambda c: pl.BlockSpec((tb, LANES), lambda b, t, c=c: (row(b, t), c // LANES))
    vec = pl.BlockSpec((1, LANES), lambda b, t: (0, 0))
    per = tb // CONV_TAIL
    prev = lambda c: pl.BlockSpec(
        (CONV_TAIL, BRANCH_W),
        lambda b, t, c=c: (b * nt * per + jnp.maximum(t * per - 1, 0), c // BRANCH_W))
    return pl.pallas_call(
        functools.partial(_dn_kernel, tb=tb),
        grid=(batch, nt),
        in_specs=[blk(ZB_DQ), blk(ZB_DK), blk(ZB_DV), prev(ZB_DQ), prev(ZB_DK), prev(ZB_DV),
                  blk(ZB_DZ), sm(ZF_S1), sm(ZF_S2),
                  pl.BlockSpec((CONV_K, 3 * BRANCH_W), lambda b, t: (0, 0)), vec, vec, vec],
        out_specs=pl.BlockSpec((tb, BRANCH_W), lambda b, t: (row(b, t), 0)),
        out_shape=jax.ShapeDtypeStruct((batch * seq, BRANCH_W), BF16),
        scratch_shapes=[pltpu.VMEM((N_HEADS, DN_DK, HEAD_W), F32)],
        compiler_params=pltpu.CompilerParams(
            dimension_semantics=("arbitrary", "arbitrary"),
            vmem_limit_bytes=VMEM_LIMIT),
        name="deltanet",
    )(zb, zb, zb, zb, zb, zb, zb, zf, zf, conv_w, alog, dt, gain)


def _merge_kernel(oa_ref, ob_ref, oc_ref, od_ref, ga_ref, gb_ref, gc_ref, gd_ref,
                  x_ref, wb_ref, wo_ref, y_ref):
    merged = None
    branches = ((oa_ref, ga_ref), (ob_ref, gb_ref), (oc_ref, gc_ref), (od_ref, gd_ref))
    for n, (o_ref, gate_ref) in enumerate(branches):
        proj = jnp.dot(o_ref[...], wb_ref[n], preferred_element_type=F32)
        term = _sigmoid(gate_ref[...].astype(F32)) * proj
        merged = term if merged is None else merged + term
    y_ref[...] = x_ref[...] + jnp.dot(_bf(merged), wo_ref[...], preferred_element_type=F32)


def _merge(oa, ob, oc, od, zb, x2, wbr, wo, tm):
    n = x2.shape[0]
    ob_spec = pl.BlockSpec((tm, BRANCH_W), lambda i: (i, 0))
    gate = lambda k: pl.BlockSpec((tm, D_MODEL), lambda i, k=k: (i, ZB_GATE // D_MODEL + k))
    return pl.pallas_call(
        _merge_kernel,
        grid=(n // tm,),
        in_specs=[ob_spec, ob_spec, ob_spec, ob_spec,
                  gate(0), gate(1), gate(2), gate(3),
                  pl.BlockSpec((tm, D_MODEL), lambda i: (i, 0)),
                  pl.BlockSpec((N_BRANCH, BRANCH_W, D_MODEL), lambda i: (0, 0, 0)),
                  pl.BlockSpec((D_MODEL, D_MODEL), lambda i: (0, 0))],
        out_specs=pl.BlockSpec((tm, D_MODEL), lambda i: (i, 0)),
        out_shape=jax.ShapeDtypeStruct((n, D_MODEL), F32),
        compiler_params=pltpu.CompilerParams(
            dimension_semantics=("arbitrary",), vmem_limit_bytes=VMEM_LIMIT),
        name="merge",
    )(oa, ob, oc, od, zb, zb, zb, zb, x2, wbr, wo)


def _mlp_kernel(x_ref, g_ref, wu_ref, wd_ref, y_ref, h_ref, acc_ref):
    j = pl.program_id(1)

    @pl.when(j == 0)
    def _():
        x = x_ref[...]
        ms = jnp.mean(x * x, axis=-1, keepdims=True)
        h_ref[...] = _bf((x * lax.rsqrt(ms + EPS)) * g_ref[...])
        acc_ref[...] = jnp.zeros(acc_ref.shape, F32)

    up = jnp.dot(h_ref[...], wu_ref[...], preferred_element_type=F32)
    act = jnp.square(jnp.maximum(up, 0.0))
    acc_ref[...] += jnp.dot(_bf(act), wd_ref[...], preferred_element_type=F32)

    @pl.when(j == pl.num_programs(1) - 1)
    def _():
        y_ref[...] = x_ref[...] + acc_ref[...]


def _mlp(x2, g, wu, wd, tm, tf):
    n = x2.shape[0]
    return pl.pallas_call(
        _mlp_kernel,
        grid=(n // tm, D_FF // tf),
        in_specs=[pl.BlockSpec((tm, D_MODEL), lambda i, j: (i, 0)),
                  pl.BlockSpec((1, D_MODEL), lambda i, j: (0, 0)),
                  pl.BlockSpec((D_MODEL, tf), lambda i, j: (0, j)),
                  pl.BlockSpec((tf, D_MODEL), lambda i, j: (j, 0))],
        out_specs=pl.BlockSpec((tm, D_MODEL), lambda i, j: (i, 0)),
        out_shape=jax.ShapeDtypeStruct((n, D_MODEL), F32),
        scratch_shapes=[pltpu.VMEM((tm, D_MODEL), BF16), pltpu.VMEM((tm, D_MODEL), F32)],
        compiler_params=pltpu.CompilerParams(
            dimension_semantics=("arbitrary", "arbitrary"), vmem_limit_bytes=VMEM_LIMIT),
        name="mlp",
    )(x2, g, wu, wd)


def _t5_bucket_table():
    n = np.arange(WINDOW)
    max_exact = N_BUCKETS // 2
    nf = np.maximum(n, max_exact).astype(np.float32)
    large = max_exact + (np.log(nf / np.float32(max_exact)) / np.float32(math.log(MAX_DISTANCE / max_exact))
                         * (N_BUCKETS - max_exact)).astype(np.int32)
    large = np.minimum(large, N_BUCKETS - 1)
    return np.where(n < max_exact, n, large)


def _swa_bias(rel_table):
    qpos = np.arange(WINDOW)[:, None] + WINDOW
    kpos = np.arange(2 * WINDOW)[None, :]
    dist = qpos - kpos
    in_window = (dist >= 0) & (dist < WINDOW)
    bucket = _t5_bucket_table()[np.clip(dist, 0, WINDOW - 1)]
    bias = rel_table.astype(F32)[bucket]
    bias = jnp.where(in_window[:, :, None], bias, NEG_BIG)
    return bias.transpose(2, 0, 1)


def _lane_row(vals, offset):
    return jnp.zeros((1, LANES), F32).at[0, offset:offset + vals.shape[0]].set(vals.astype(F32))


def _layout_w_in(w):
    sizes = (512, 512, 512, 512, 256, 256, 512, 4, 4, 512, 512, 128, 128, 1536, 4, 4, 512, 4096)
    parts, start = [], 0
    for s in sizes:
        parts.append(w[:, start:start + s])
        start += s
    (hq, hf, hi, hg, mq, mk, mv, mi, mf, mo, aq, ak, av, dqkv, db, da, dz, gate) = parts
    zpad = lambda n: jnp.zeros((w.shape[0], n), w.dtype)
    wb = jnp.concatenate([hq, hi, hg, mq, mk, mv, mo, aq, ak, av, zpad(256), dqkv, dz, gate], axis=1)
    wf = jnp.concatenate([hf, mi, db, zpad(LANES - 8), mf, da, zpad(LANES - 8)], axis=1)
    return _bf(wb), _bf(wf)


def kernel(x, norm_mix_g, w_in, hgrn_lb_table, hgrn_out_g, mlstm_if_bias, mlstm_out_g,
           attn_q_norm_g, attn_k_norm_g, attn_sinks, rel_bias_table, dn_conv_w, dn_a_log,
           dn_dt_bias, dn_out_g, w_branch, w_out, norm_mlp_g, w_up, w_down):
    batch, seq, d = x.shape
    depth = w_in.shape[0]
    n = batch * seq
    tb = 256
    assert seq % tb == 0 and d == D_MODEL
    tm = 1024 if n % 1024 == 0 else 256

    lb_p = jax.nn.softmax(hgrn_lb_table.astype(F32), axis=0)
    lower_bounds = jnp.cumsum(lb_p, axis=0) - lb_p[0]
    bias = _swa_bias(rel_bias_table)

    x2 = x.reshape(n, d)
    for l in range(depth):
        wb, wf = _layout_w_in(w_in[l])
        zb, zf = _inproj(x2, norm_mix_g[l].reshape(1, d), wb, wf, tm, 2048)

        o_a = _hgrn(zb, zf, lower_bounds[l].reshape(1, -1), hgrn_out_g[l].reshape(1, -1).astype(F32),
                    batch, seq, tb)
        o_b = _mlstm(zb, zf, _lane_row(mlstm_if_bias[l, 0], 0), _lane_row(mlstm_if_bias[l, 1], 0),
                     mlstm_out_g[l].reshape(1, -1).astype(F32), batch, seq, tb)
        o_c = _swa(zb, jnp.tile(attn_q_norm_g[l].astype(F32), AT_HEADS).reshape(1, -1),
                   jnp.tile(attn_k_norm_g[l].astype(F32), 2).reshape(1, -1),
                   jnp.broadcast_to(attn_sinks[l].astype(F32)[:, None], (AT_HEADS, LANES)),
                   bias, batch, seq)
        o_d = _deltanet(zb, zf, dn_conv_w[l].astype(F32), _lane_row(dn_a_log[l], N_HEADS),
                        _lane_row(dn_dt_bias[l], N_HEADS), dn_out_g[l].reshape(1, -1).astype(F32),
                        batch, seq, tb)

        x2 = _merge(o_a, o_b, o_c, o_d, zb, x2, _bf(w_branch[l]), _bf(w_out[l]), 512 if n % 512 == 0 else 256)
        x2 = _mlp(x2, norm_mlp_g[l].reshape(1, d), _bf(w_up[l]), _bf(w_down[l]), tm, 1024)
    return x2.reshape(batch, seq, d)
```

```python
import functools
import math

import numpy as np
import jax
import jax.numpy as jnp
from jax import lax
from jax.experimental import pallas as pl
from jax.experimental.pallas import tpu as pltpu

F32 = jnp.float32
BF16 = jnp.bfloat16

D_MODEL = 1024
EPS = 1e-6
CHUNK = 64
NEG_BIG = -1e30
TINY = 1e-30
LANES = 128
HEAD_W = 128
N_HEADS = 4
BRANCH_W = 512
N_BRANCH = 4
D_FF = 4 * D_MODEL

AT_HEADS = 8
AT_HD = 64
WINDOW = 128
N_BUCKETS = 32
MAX_DISTANCE = 128
CONV_K = 4
ML_DK = 64
DN_DK = 128
CHUNK_SHIFT = 6
AT_HD_SHIFT = 6
LOG2_E = 1.4426950408889634
CONV_TAIL = 16

ZB_HQ, ZB_HI, ZB_HG = 0, 512, 1024
ZB_MQK, ZB_MV, ZB_MO = 1536, 2048, 2560
ZB_AQ, ZB_AKV = 3072, 3584
ZB_DQ, ZB_DK, ZB_DV, ZB_DZ = 4096, 4608, 5120, 5632
ZB_GATE = 6144
ZB_W = 10240
ZF_HF, ZF_S1, ZF_S2 = 0, 512, 640
ZF_W = 768

VMEM_LIMIT = 56 * 1024 * 1024


def _bf(x):
    return x.astype(BF16)


def _mm(a, b):
    return jnp.dot(_bf(a), _bf(b), preferred_element_type=F32)


def _mm_nt(a, b):
    return lax.dot_general(_bf(a), _bf(b), (((1,), (1,)), ((), ())),
                           preferred_element_type=F32)


def _mm_tn(a, b):
    return lax.dot_general(_bf(a), _bf(b), (((0,), (0,)), ((), ())),
                           preferred_element_type=F32)


def _split3(x):
    hi = _bf(x)
    r = x - hi.astype(F32)
    mid = _bf(r)
    lo = _bf(r - mid.astype(F32))
    return hi, mid, lo


def _mm3(sel, x):
    hi, mid, lo = _split3(x)
    return (jnp.dot(sel, hi, preferred_element_type=F32)
            + jnp.dot(sel, mid, preferred_element_type=F32)
            + jnp.dot(sel, lo, preferred_element_type=F32))


def _mm3_nt(sel, x):
    dn = (((1,), (1,)), ((), ()))
    hi, mid, lo = _split3(x)
    return (lax.dot_general(sel, hi, dn, preferred_element_type=F32)
            + lax.dot_general(sel, mid, dn, preferred_element_type=F32)
            + lax.dot_general(sel, lo, dn, preferred_element_type=F32))


def _sigmoid(x):
    return 1.0 / (1.0 + jnp.exp(-x))


def _silu(x):
    return x * _sigmoid(x)


def _log1pexp_neg_abs(x):
    return jnp.log(1.0 + jnp.exp(-jnp.abs(x)))


def _log_sigmoid(x):
    return jnp.minimum(x, 0.0) - _log1pexp_neg_abs(x)


def _softplus(x):
    return jnp.maximum(x, 0.0) + _log1pexp_neg_abs(x)


def _iota(shape, dim):
    return lax.broadcasted_iota(jnp.int32, shape, dim)


def _chunk_tril(n):
    t = _iota((n, n), 0)
    s = _iota((n, n), 1)
    return ((t >> CHUNK_SHIFT) == (s >> CHUNK_SHIFT)) & (s <= t)


def _head_rms(o):
    return o * lax.rsqrt(jnp.mean(o * o, axis=-1, keepdims=True) + EPS)


def _seg_ones(n, seg_shift):
    a = _iota((n, n), 0) >> seg_shift
    b = _iota((n, n), 1) >> seg_shift
    return jnp.where(a == b, 1.0, 0.0).astype(BF16)


def _inproj_kernel(x_ref, g_ref, wb_ref, wf_ref, zb_ref, zf_ref, h_ref):
    @pl.when(pl.program_id(1) == 0)
    def _():
        x = x_ref[...]
        ms = jnp.mean(x * x, axis=-1, keepdims=True)
        h = _bf((x * lax.rsqrt(ms + EPS)) * g_ref[...])
        h_ref[...] = h
        zf_ref[...] = jnp.dot(h, wf_ref[...], preferred_element_type=F32)

    zb_ref[...] = _bf(jnp.dot(h_ref[...], wb_ref[...], preferred_element_type=F32))


def _inproj(x2, g, wb, wf, tm, tn):
    n, d = x2.shape
    return pl.pallas_call(
        _inproj_kernel,
        grid=(n // tm, ZB_W // tn),
        in_specs=[
            pl.BlockSpec((tm, d), lambda i, j: (i, 0)),
            pl.BlockSpec((1, d), lambda i, j: (0, 0)),
            pl.BlockSpec((d, tn), lambda i, j: (0, j)),
            pl.BlockSpec((d, ZF_W), lambda i, j: (0, 0)),
        ],
        out_specs=[
            pl.BlockSpec((tm, tn), lambda i, j: (i, j)),
            pl.BlockSpec((tm, ZF_W), lambda i, j: (i, 0)),
        ],
        out_shape=[jax.ShapeDtypeStruct((n, ZB_W), BF16),
                   jax.ShapeDtypeStruct((n, ZF_W), F32)],
        scratch_shapes=[pltpu.VMEM((tm, d), BF16)],
        compiler_params=pltpu.CompilerParams(
            dimension_semantics=("arbitrary", "arbitrary"),
            vmem_limit_bytes=VMEM_LIMIT),
        name="inproj",
    )(x2, g, wb, wf)


def _hgrn_kernel(q_ref, v_ref, g_ref, f_ref, lb_ref, og_ref, o_ref, st_ref, *, tb):
    @pl.when(pl.program_id(1) == 0)
    def _():
        st_ref[...] = jnp.zeros(st_ref.shape, F32)

    n_chunks = tb // CHUNK
    z = f_ref[...]
    lb = lb_ref[...]
    f = lb + (1.0 - lb) * _sigmoid(z)
    logf2 = jnp.log(jnp.maximum(f, TINY)) * LOG2_E
    k = (1.0 - lb) * _sigmoid(-z)
    q = _silu(q_ref[...].astype(F32))

    t_sq = _iota((tb, tb), 0)
    r_sq = _iota((tb, tb), 1)
    pre = [((t_sq >> lv) == (r_sq >> lv)) & (r_sq <= t_sq) for lv in range(1, CHUNK_SHIFT + 1)]
    suf = [((t_sq >> lv) == (r_sq >> lv)) & (r_sq > t_sq) for lv in range(1, CHUNK_SHIFT + 1)]
    tri_all = jnp.concatenate([jnp.where(m, 1.0, 0.0).astype(BF16) for m in pre + suf], axis=0)
    hi = _bf(logf2)
    lo = _bf(logf2 - hi.astype(F32))
    sums = (jnp.dot(tri_all, hi, preferred_element_type=F32)
            + jnp.dot(tri_all, lo, preferred_element_type=F32))
    n_lv = CHUNK_SHIFT
    prefix = [logf2] + [sums[i * tb:(i + 1) * tb] for i in range(n_lv)]
    suffix = [None] + [sums[(n_lv + i) * tb:(n_lv + i + 1) * tb] for i in range(n_lv)]

    row = _iota((tb, 1), 0)
    qd, kd = [], []
    for lv in range(n_lv):
        upper = ((row >> lv) & 1) == 1
        qd.append(_bf(q * jnp.exp2(jnp.where(upper, prefix[lv], NEG_BIG))))
        kd.append(_bf(jnp.where(upper, 0.0, k) if lv == 0
                      else k * jnp.exp2(jnp.where(upper, NEG_BIG, suffix[lv]))))
    q_in = _bf(q * jnp.exp2(prefix[n_lv]))
    k_out = _bf(k * jnp.exp2(suffix[n_lv]))
    qk = _bf(q * k)
    ones_b = jnp.ones((HEAD_W, CHUNK), BF16)
    t_i = _iota((CHUNK, CHUNK), 0)
    s_i = _iota((CHUNK, CHUNK), 1)
    block_masks = [(t_i >> (lv + 1)) == (s_i >> (lv + 1)) for lv in range(n_lv)]
    eye = t_i == s_i

    heads = range(N_HEADS)
    cs = [slice(h * HEAD_W, (h + 1) * HEAD_W) for h in heads]
    intra = [[None] * N_HEADS for _ in range(n_chunks)]
    for ci in range(n_chunks):
        rs = slice(ci * CHUNK, (ci + 1) * CHUNK)
        a = [jnp.where(eye, jnp.dot(qk[rs, cs[h]], ones_b, preferred_element_type=F32), 0.0)
             for h in heads]
        for lv in range(n_lv):
            a = [a[h] + jnp.where(block_masks[lv], _mm_nt(qd[lv][rs, cs[h]], kd[lv][rs, cs[h]]), 0.0)
                 for h in heads]
        for h in heads:
            intra[ci][h] = _mm(a[h], v_ref[rs, cs[h]])

    st = [st_ref[h] for h in heads]
    for ci in range(n_chunks):
        rs = slice(ci * CHUNK, (ci + 1) * CHUNK)
        for h in heads:
            o = intra[ci][h] + _mm_nt(q_in[rs, cs[h]], st[h])
            total = prefix[n_lv][(ci + 1) * CHUNK - 1:(ci + 1) * CHUNK, cs[h]]
            st[h] = jnp.exp2(total) * st[h] + _mm_tn(v_ref[rs, cs[h]], k_out[rs, cs[h]])
            o = _head_rms(o) * og_ref[:, cs[h]] * _silu(g_ref[rs, cs[h]].astype(F32))
            o_ref[rs, cs[h]] = _bf(o)
    for h in heads:
        st_ref[h] = st[h]


def _hgrn(zb, zf, lb, og, batch, seq, tb):
    nt = seq // tb
    row = lambda b, t: b * nt + t
    blk = lambda c: pl.BlockSpec((tb, BRANCH_W), lambda b, t, c=c: (row(b, t), c // BRANCH_W))
    vec = pl.BlockSpec((1, BRANCH_W), lambda b, t: (0, 0))
    return pl.pallas_call(
        functools.partial(_hgrn_kernel, tb=tb),
        grid=(batch, nt),
        in_specs=[blk(ZB_HQ), blk(ZB_HI), blk(ZB_HG), blk(ZF_HF), vec, vec],
        out_specs=pl.BlockSpec((tb, BRANCH_W), lambda b, t: (row(b, t), 0)),
        out_shape=jax.ShapeDtypeStruct((batch * seq, BRANCH_W), BF16),
        scratch_shapes=[pltpu.VMEM((N_HEADS, HEAD_W, HEAD_W), F32)],
        compiler_params=pltpu.CompilerParams(
            dimension_semantics=("arbitrary", "arbitrary"),
            vmem_limit_bytes=VMEM_LIMIT),
        name="hgrn2",
    )(zb, zb, zb, zf, lb, og)


def _mlstm_kernel(qk_ref, v_ref, og_pre_ref, s1_ref, s2_ref, b0_ref, b1_ref, gain_ref,
                  o_ref, c_ref, m_ref, *, tb):
    @pl.when(pl.program_id(1) == 0)
    def _():
        c_ref[...] = jnp.zeros(c_ref.shape, F32)
        m_ref[...] = jnp.zeros(m_ref.shape, F32)

    li_all = s1_ref[...] + b0_ref[...]
    lf_all = _log_sigmoid(s2_ref[...] + b1_ref[...])
    tri = jnp.where(_chunk_tril(tb), 1.0, 0.0).astype(BF16)
    cum_all = _mm3(tri, lf_all)
    rr_all = li_all - cum_all
    eye = jnp.where(_iota((16, LANES), 0) == _iota((16, LANES), 1), 1.0, 0.0).astype(BF16)
    causal = _iota((CHUNK, CHUNK), 1) <= _iota((CHUNK, CHUNK), 0)
    lane = _iota((1, LANES), 1)
    e0 = jnp.where(_iota((CHUNK, LANES), 1) == 0, 1.0, 0.0).astype(BF16)

    n_chunks = tb // CHUNK
    pre = [[None] * N_HEADS for _ in range(n_chunks)]
    for ci in range(n_chunks):
        rs = slice(ci * CHUNK, (ci + 1) * CHUNK)
        rr_t = _mm3_nt(eye, rr_all[rs])
        for h in range(N_HEADS):
            pair = h // 2
            half = (lane >> AT_HD_SHIFT) == (h % 2)
            gs = slice(pair * LANES, (pair + 1) * LANES)
            qm = jnp.where(half, qk_ref[rs, gs], jnp.zeros((), BF16))
            ks = qk_ref[rs, 2 * LANES + pair * LANES:2 * LANES + (pair + 1) * LANES].astype(F32) * (ML_DK ** -0.5)
            vs = slice(h * HEAD_W, (h + 1) * HEAD_W)
            v_aug = jnp.concatenate([v_ref[rs, vs], e0], axis=1)
            cum_col = cum_all[rs, h:h + 1]
            li_col = li_all[rs, h:h + 1]
            logd = jnp.where(causal, cum_col + rr_t[h:h + 1, :], NEG_BIG)
            rowmax = jnp.max(logd, axis=-1, keepdims=True)
            p0 = _mm(_mm_nt(qm, ks) * jnp.exp(logd - rowmax), v_aug)
            mref = rowmax[CHUNK - 1:CHUNK, :]
            cum_last = cum_col[CHUNK - 1:CHUNK, :]
            upd = _mm_tn(ks * jnp.exp(cum_last - cum_col + li_col - mref), v_aug)
            pre[ci][h] = (qm, cum_col, rowmax, p0, mref, cum_last, upd)

    cms = [c_ref[h] for h in range(N_HEADS)]
    ms = [m_ref[h:h + 1, 0:1] for h in range(N_HEADS)]
    for ci in range(n_chunks):
        rs = slice(ci * CHUNK, (ci + 1) * CHUNK)
        for h in range(N_HEADS):
            vs = slice(h * HEAD_W, (h + 1) * HEAD_W)
            qm, cum_col, rowmax, p0, mref, cum_last, upd = pre[ci][h]
            m_old = ms[h]
            m_inter = cum_col + m_old
            m_t = jnp.maximum(m_inter, rowmax)
            nd = jnp.exp(rowmax - m_t) * p0 + jnp.exp(m_inter - m_t) * _mm(qm, cms[h])
            num = nd[:, :HEAD_W]
            den = nd[:, HEAD_W:HEAD_W + 1]
            hval = num / jnp.maximum(jnp.abs(den), jnp.exp(-m_t))
            m_new = m_t[CHUNK - 1:CHUNK, :]
            cms[h] = jnp.exp(cum_last + m_old - m_new) * cms[h] + jnp.exp(mref - m_new) * upd
            ms[h] = m_new
            out = _head_rms(hval) * gain_ref[:, vs] * _sigmoid(og_pre_ref[rs, vs].astype(F32))
            o_ref[rs, vs] = _bf(out)
    for h in range(N_HEADS):
        c_ref[h] = cms[h]
        m_ref[h:h + 1, :] = jnp.broadcast_to(ms[h], (1, LANES))


def _mlstm(zb, zf, b0, b1, gain, batch, seq, tb):
    nt = seq // tb
    row = lambda b, t: b * nt + t
    blk = lambda c: pl.BlockSpec((tb, BRANCH_W), lambda b, t, c=c: (row(b, t), c // BRANCH_W))
    sm = lambda c: pl.BlockSpec((tb, LANES), lambda b, t, c=c: (row(b, t), c // LANES))
    return pl.pallas_call(
        functools.partial(_mlstm_kernel, tb=tb),
        grid=(batch, nt),
        in_specs=[blk(ZB_MQK), blk(ZB_MV), blk(ZB_MO), sm(ZF_S1), sm(ZF_S2),
                  pl.BlockSpec((1, LANES), lambda b, t: (0, 0)),
                  pl.BlockSpec((1, LANES), lambda b, t: (0, 0)),
                  pl.BlockSpec((1, BRANCH_W), lambda b, t: (0, 0))],
        out_specs=pl.BlockSpec((tb, BRANCH_W), lambda b, t: (row(b, t), 0)),
        out_shape=jax.ShapeDtypeStruct((batch * seq, BRANCH_W), BF16),
        scratch_shapes=[pltpu.VMEM((N_HEADS, LANES, 2 * HEAD_W), F32),
                        pltpu.VMEM((8, LANES), F32)],
        compiler_params=pltpu.CompilerParams(
            dimension_semantics=("arbitrary", "arbitrary"),
            vmem_limit_bytes=VMEM_LIMIT),
        name="mlstm",
    )(zb, zb, zb, zf, zf, b0, b1, gain)


def _swa_kernel(q_ref, kvp_ref, kvc_ref, qg_ref, kg_ref, sink_ref, bias_ref, o_ref, *, n_win):
    first_cols = jnp.where(pl.program_id(1) == 0, WINDOW, 0)

    qf = q_ref[...].astype(F32)
    ms_q = _mm(qf * qf, _seg_ones(BRANCH_W, AT_HD_SHIFT)) * (1.0 / AT_HD)
    qn = _bf(qf * lax.rsqrt(ms_q + EPS) * qg_ref[...] * (AT_HD ** -0.5))

    kw = jnp.concatenate([kvp_ref[:, :LANES], kvc_ref[:, :LANES]], axis=0).astype(F32)
    vw = jnp.concatenate([kvp_ref[:, LANES:], kvc_ref[:, LANES:]], axis=0).astype(F32)
    ms_k = _mm(kw * kw, _seg_ones(LANES, AT_HD_SHIFT)) * (1.0 / AT_HD)
    kn = kw * lax.rsqrt(ms_k + EPS) * kg_ref[...]
    lane = _iota((1, LANES), 1)
    halves = [(lane >> AT_HD_SHIFT) == i for i in range(2)]
    k_src = {True: _bf(kn), False: _bf(pltpu.roll(kn, AT_HD, 1))}
    v_roll = pltpu.roll(vw, AT_HD, 1)
    v_src = {(same, i): _bf(jnp.where(halves[i], vw if same else v_roll, 0.0))
             for same in (True, False) for i in range(2)}

    kpos = _iota((1, 2 * WINDOW), 1)
    for w in range(n_win):
        qs = slice(w * WINDOW, (w + 1) * WINDOW)
        ws = slice(w * WINDOW, (w + 2) * WINDOW)
        for pair in range(AT_HEADS // 2):
            acc = jnp.zeros((WINDOW, LANES), F32)
            for half_i in range(2):
                h = 2 * pair + half_i
                same = half_i == h // (AT_HEADS // 2)
                qh = jnp.where(halves[half_i], qn[qs, pair * LANES:(pair + 1) * LANES], jnp.zeros((), BF16))
                lg = _mm_nt(qh, k_src[same][ws]) + bias_ref[h]
                if w == 0:
                    lg = jnp.where(kpos < first_cols, NEG_BIG, lg)
                sink = sink_ref[h:h + 1, 0:1]
                mx = jnp.maximum(jnp.max(lg, axis=-1, keepdims=True), sink)
                p = jnp.exp(lg - mx)
                denom = jnp.sum(p, axis=-1, keepdims=True) + jnp.exp(sink - mx)
                acc = acc + _mm(p, v_src[(same, half_i)][ws]) / denom
            o_ref[qs, pair * LANES:(pair + 1) * LANES] = _bf(acc)


def _swa(zb, qg, kg, sinks, bias, batch, seq, n_win):
    tb = n_win * WINDOW
    nt = seq // tb
    row = lambda b, t: b * nt + t
    kvw = 2 * LANES
    return pl.pallas_call(
        functools.partial(_swa_kernel, n_win=n_win),
        grid=(batch, nt),
        in_specs=[
            pl.BlockSpec((tb, BRANCH_W), lambda b, t: (row(b, t), ZB_AQ // BRANCH_W)),
            pl.BlockSpec((WINDOW, kvw),
                         lambda b, t: (b * nt * n_win + jnp.maximum(t * n_win - 1, 0), ZB_AKV // kvw)),
            pl.BlockSpec((tb, kvw), lambda b, t: (row(b, t), ZB_AKV // kvw)),
            pl.BlockSpec((1, BRANCH_W), lambda b, t: (0, 0)),
            pl.BlockSpec((1, LANES), lambda b, t: (0, 0)),
            pl.BlockSpec((AT_HEADS, LANES), lambda b, t: (0, 0)),
            pl.BlockSpec((AT_HEADS, WINDOW, 2 * WINDOW), lambda b, t: (0, 0, 0)),
        ],
        out_specs=pl.BlockSpec((tb, BRANCH_W), lambda b, t: (row(b, t), 0)),
        out_shape=jax.ShapeDtypeStruct((batch * seq, BRANCH_W), BF16),
        compiler_params=pltpu.CompilerParams(
            dimension_semantics=("arbitrary", "arbitrary"),
            vmem_limit_bytes=VMEM_LIMIT),
        name="swa",
    )(zb, zb, zb, qg, kg, sinks, bias)


def _conv_silu(x_ref, prev_ref, w_ref, idx, has_prev):
    tb = x_ref.shape[0]
    x = x_ref[...].astype(F32)
    prev = prev_ref[CONV_TAIL - 8:, :].astype(F32) * has_prev
    x2 = jnp.concatenate([prev, x], axis=0)
    w = w_ref[:, idx * BRANCH_W:(idx + 1) * BRANCH_W]
    y = x * w[CONV_K - 1:CONV_K, :]
    for k in range(1, CONV_K):
        y = y + x2[8 - k:8 - k + tb, :] * w[CONV_K - 1 - k:CONV_K - k, :]
    return _silu(y)


def _dn_kernel(q_ref, k_ref, v_ref, qp_ref, kp_ref, vp_ref, z_ref, s1_ref, s2_ref, w_ref,
               alog_ref, dt_ref, gain_ref, o_ref, st_ref, *, tb):
    @pl.when(pl.program_id(1) == 0)
    def _():
        st_ref[...] = jnp.zeros(st_ref.shape, F32)

    n_chunks = tb // CHUNK
    has_prev = jnp.where(pl.program_id(1) > 0, 1.0, 0.0)
    q = _conv_silu(q_ref, qp_ref, w_ref, 0, has_prev)
    k = _conv_silu(k_ref, kp_ref, w_ref, 1, has_prev)
    v = _conv_silu(v_ref, vp_ref, w_ref, 2, has_prev)
    seg = _seg_ones(BRANCH_W, 7)
    q = q * lax.rsqrt(_mm(q * q, seg) + EPS) * (DN_DK ** -0.5)
    k = k * lax.rsqrt(_mm(k * k, seg) + EPS)

    beta_all = _sigmoid(s1_ref[...])
    g_all = -jnp.exp(alog_ref[...]) * _softplus(s2_ref[...] + dt_ref[...])
    incl = _chunk_tril(tb)
    tri = jnp.where(incl, 1.0, 0.0).astype(BF16)
    gam_all = _mm3(tri, g_all)
    eye = jnp.where(_iota((16, LANES), 0) == _iota((16, LANES), 1), 1.0, 0.0).astype(BF16)
    gam_t = _mm3_nt(eye, gam_all)
    t_sq = _iota((tb, tb), 0)
    s_sq = _iota((tb, tb), 1)
    level_masks = []
    for lv in range(CHUNK_SHIFT):
        m = 1 << lv
        level_masks.append(((t_sq >> (lv + 1)) == (s_sq >> (lv + 1)))
                           & ((t_sq & m) != 0) & ((s_sq & m) == 0))

    heads = range(N_HEADS)
    cs = [slice(h * HEAD_W, (h + 1) * HEAD_W) for h in heads]
    gcol = [gam_all[:, N_HEADS + h:N_HEADS + h + 1] for h in heads]
    bcol = [beta_all[:, N_HEADS + h:N_HEADS + h + 1] for h in heads]
    decay = [jnp.exp(jnp.where(incl, gcol[h] - gam_t[N_HEADS + h:N_HEADS + h + 1, :], NEG_BIG))
             for h in heads]
    kb = [_bf(k[:, cs[h]]) for h in heads]
    low = [bcol[h] * _mm_nt(kb[h], kb[h]) * decay[h] for h in heads]
    n_mat = [-jnp.where(level_masks[0], low[h], 0.0) for h in heads]
    for lm in level_masks[1:]:
        l_m = [jnp.where(lm, low[h], 0.0) for h in heads]
        x_m = [l_m[h] + _mm(l_m[h], n_mat[h]) for h in heads]
        n_mat = [n_mat[h] - x_m[h] - _mm(n_mat[h], x_m[h]) for h in heads]
    rhs = [jnp.concatenate([v[:, cs[h]] * bcol[h], k[:, cs[h]] * (bcol[h] * jnp.exp(gcol[h]))], axis=1)
           for h in heads]
    sol = [rhs[h] + _mm(n_mat[h], rhs[h]) for h in heads]
    attn = [_bf(_mm_nt(q[:, cs[h]], kb[h]) * decay[h]) for h in heads]
    q_dec = [q[:, cs[h]] * jnp.exp(gcol[h]) for h in heads]

    st = [st_ref[h] for h in heads]
    vnew = [[] for _ in heads]
    for ci in range(n_chunks):
        rs = slice(ci * CHUNK, (ci + 1) * CHUNK)
        pad = [jnp.zeros((CHUNK, HEAD_W), BF16)] * (n_chunks - ci - 1)
        for h in heads:
            g_last = gcol[h][ci * CHUNK + CHUNK - 1:(ci + 1) * CHUNK, :]
            ws = _mm(jnp.concatenate([sol[h][rs, HEAD_W:], q_dec[h][rs]], axis=0), st[h])
            v_new = sol[h][rs, :HEAD_W] - ws[:CHUNK]
            vnew[h].append(_bf(v_new))
            v_full = jnp.concatenate(vnew[h] + pad, axis=0)
            o = ws[CHUNK:] + jnp.dot(attn[h][rs], v_full, preferred_element_type=F32)
            k_dec = k[rs, cs[h]] * jnp.exp(g_last - gcol[h][rs])
            st[h] = jnp.exp(g_last) * st[h] + _mm_tn(k_dec, v_new)
            out = _head_rms(o) * gain_ref[...] * _silu(z_ref[rs, cs[h]].astype(F32))
            o_ref[rs, cs[h]] = _bf(out)
    for h in heads:
        st_ref[h] = st[h]


def _deltanet(zb, zf, conv_w, alog, dt, gain, batch, seq, tb):
    nt = seq // tb
    row = lambda b, t: b * nt + t
    blk = lambda c: pl.BlockSpec((tb, BRANCH_W), lambda b, t, c=c: (row(b, t), c // BRANCH_W))
    sm = lambda c: pl.BlockSpec((tb, LANES), lambda b, t, c=c: (row(b, t), c // LANES))
    vec = pl.BlockSpec((1, LANES), lambda b, t: (0, 0))
    per = tb // CONV_TAIL
    prev = lambda c: pl.BlockSpec(
        (CONV_TAIL, BRANCH_W),
        lambda b, t, c=c: (b * nt * per + jnp.maximum(t * per - 1, 0), c // BRANCH_W))
    return pl.pallas_call(
        functools.partial(_dn_kernel, tb=tb),
        grid=(batch, nt),
        in_specs=[blk(ZB_DQ), blk(ZB_DK), blk(ZB_DV), prev(ZB_DQ), prev(ZB_DK), prev(ZB_DV),
                  blk(ZB_DZ), sm(ZF_S1), sm(ZF_S2),
                  pl.BlockSpec((CONV_K, 3 * BRANCH_W), lambda b, t: (0, 0)), vec, vec, vec],
        out_specs=pl.BlockSpec((tb, BRANCH_W), lambda b, t: (row(b, t), 0)),
        out_shape=jax.ShapeDtypeStruct((batch * seq, BRANCH_W), BF16),
        scratch_shapes=[pltpu.VMEM((N_HEADS, DN_DK, HEAD_W), F32)],
        compiler_params=pltpu.CompilerParams(
            dimension_semantics=("arbitrary", "arbitrary"),
            vmem_limit_bytes=VMEM_LIMIT),
        name="deltanet",
    )(zb, zb, zb, zb, zb, zb, zb, zf, zf, conv_w, alog, dt, gain)


def _merge_kernel(oa_ref, ob_ref, oc_ref, od_ref, ga_ref, gb_ref, gc_ref, gd_ref,
                  x_ref, wb_ref, wo_ref, y_ref):
    merged = None
    branches = ((oa_ref, ga_ref), (ob_ref, gb_ref), (oc_ref, gc_ref), (od_ref, gd_ref))
    for n, (o_ref, gate_ref) in enumerate(branches):
        proj = jnp.dot(o_ref[...], wb_ref[n], preferred_element_type=F32)
        term = _sigmoid(gate_ref[...].astype(F32)) * proj
        merged = term if merged is None else merged + term
    y_ref[...] = x_ref[...] + jnp.dot(_bf(merged), wo_ref[...], preferred_element_type=F32)


def _merge(oa, ob, oc, od, zb, x2, wbr, wo, tm):
    n = x2.shape[0]
    ob_spec = pl.BlockSpec((tm, BRANCH_W), lambda i: (i, 0))
    gate = lambda k: pl.BlockSpec((tm, D_MODEL), lambda i, k=k: (i, ZB_GATE // D_MODEL + k))
    return pl.pallas_call(
        _merge_kernel,
        grid=(n // tm,),
        in_specs=[ob_spec, ob_spec, ob_spec, ob_spec,
                  gate(0), gate(1), gate(2), gate(3),
                  pl.BlockSpec((tm, D_MODEL), lambda i: (i, 0)),
                  pl.BlockSpec((N_BRANCH, BRANCH_W, D_MODEL), lambda i: (0, 0, 0)),
                  pl.BlockSpec((D_MODEL, D_MODEL), lambda i: (0, 0))],
        out_specs=pl.BlockSpec((tm, D_MODEL), lambda i: (i, 0)),
        out_shape=jax.ShapeDtypeStruct((n, D_MODEL), F32),
        compiler_params=pltpu.CompilerParams(
            dimension_semantics=("arbitrary",), vmem_limit_bytes=VMEM_LIMIT),
        name="merge",
    )(oa, ob, oc, od, zb, zb, zb, zb, x2, wbr, wo)


def _mlp_kernel(x_ref, g_ref, wu_ref, wd_ref, y_ref, h_ref, acc_ref):
    j = pl.program_id(1)

    @pl.when(j == 0)
    def _():
        x = x_ref[...]
        ms = jnp.mean(x * x, axis=-1, keepdims=True)
        h_ref[...] = _bf((x * lax.rsqrt(ms + EPS)) * g_ref[...])
        acc_ref[...] = jnp.zeros(acc_ref.shape, F32)

    up = jnp.dot(h_ref[...], wu_ref[...], preferred_element_type=F32)
    act = jnp.square(jnp.maximum(up, 0.0))
    acc_ref[...] += jnp.dot(_bf(act), wd_ref[...], preferred_element_type=F32)

    @pl.when(j == pl.num_programs(1) - 1)
    def _():
        y_ref[...] = x_ref[...] + acc_ref[...]


def _mlp(x2, g, wu, wd, tm, tf):
    n = x2.shape[0]
    return pl.pallas_call(
        _mlp_kernel,
        grid=(n // tm, D_FF // tf),
        in_specs=[pl.BlockSpec((tm, D_MODEL), lambda i, j: (i, 0)),
                  pl.BlockSpec((1, D_MODEL), lambda i, j: (0, 0)),
                  pl.BlockSpec((D_MODEL, tf), lambda i, j: (0, j)),
                  pl.BlockSpec((tf, D_MODEL), lambda i, j: (j, 0))],
        out_specs=pl.BlockSpec((tm, D_MODEL), lambda i, j: (i, 0)),
        out_shape=jax.ShapeDtypeStruct((n, D_MODEL), F32),
        scratch_shapes=[pltpu.VMEM((tm, D_MODEL), BF16), pltpu.VMEM((tm, D_MODEL), F32)],
        compiler_params=pltpu.CompilerParams(
            dimension_semantics=("arbitrary", "arbitrary"), vmem_limit_bytes=VMEM_LIMIT),
        name="mlp",
    )(x2, g, wu, wd)


def _t5_bucket_table():
    n = np.arange(WINDOW)
    max_exact = N_BUCKETS // 2
    nf = np.maximum(n, max_exact).astype(np.float32)
    large = max_exact + (np.log(nf / np.float32(max_exact)) / np.float32(math.log(MAX_DISTANCE / max_exact))
                         * (N_BUCKETS - max_exact)).astype(np.int32)
    large = np.minimum(large, N_BUCKETS - 1)
    return np.where(n < max_exact, n, large)


def _swa_bias(rel_table):
    per_dist = rel_table.astype(F32)[_t5_bucket_table()].T
    n_heads = per_dist.shape[0]
    span = 3 * WINDOW
    pad_lo = jnp.full((n_heads, WINDOW - 1), NEG_BIG, F32)
    pad_hi = jnp.full((n_heads, span - 2 * WINDOW + 1), NEG_BIG, F32)
    v = jnp.concatenate([pad_lo, per_dist, pad_hi], axis=1)
    hank = jnp.tile(v, (1, WINDOW + 1))[:, :WINDOW * (span + 1)].reshape(n_heads, WINDOW, span + 1)
    return hank[:, :, :2 * WINDOW][:, :, ::-1]


def _lane_row(vals, offset):
    return jnp.zeros((1, LANES), F32).at[0, offset:offset + vals.shape[0]].set(vals.astype(F32))


def _layout_w_in(w):
    sizes = (512, 512, 512, 512, 256, 256, 512, 4, 4, 512, 512, 128, 128, 1536, 4, 4, 512, 4096)
    parts, start = [], 0
    for s in sizes:
        parts.append(w[:, start:start + s])
        start += s
    (hq, hf, hi, hg, mq, mk, mv, mi, mf, mo, aq, ak, av, dqkv, db, da, dz, gate) = parts
    zpad = lambda n: jnp.zeros((w.shape[0], n), w.dtype)
    wb = jnp.concatenate([hq, hi, hg, mq, mk, mv, mo, aq, ak, av, zpad(256), dqkv, dz, gate], axis=1)
    wf = jnp.concatenate([hf, mi, db, zpad(LANES - 8), mf, da, zpad(LANES - 8)], axis=1)
    return _bf(wb), _bf(wf)


def kernel(x, norm_mix_g, w_in, hgrn_lb_table, hgrn_out_g, mlstm_if_bias, mlstm_out_g,
           attn_q_norm_g, attn_k_norm_g, attn_sinks, rel_bias_table, dn_conv_w, dn_a_log,
           dn_dt_bias, dn_out_g, w_branch, w_out, norm_mlp_g, w_up, w_down):
    batch, seq, d = x.shape
    depth = w_in.shape[0]
    n = batch * seq
    tb = 256
    assert seq % tb == 0 and d == D_MODEL
    tm = 1024 if n % 1024 == 0 else 256

    lb_p = jax.nn.softmax(hgrn_lb_table.astype(F32), axis=0)
    lower_bounds = jnp.cumsum(lb_p, axis=0) - lb_p[0]
    bias = _swa_bias(rel_bias_table)

    x2 = x.reshape(n, d)
    for l in range(depth):
        wb, wf = _layout_w_in(w_in[l])
        zb, zf = _inproj(x2, norm_mix_g[l].reshape(1, d), wb, wf, tm, 2048)

        o_a = _hgrn(zb, zf, lower_bounds[l].reshape(1, -1), hgrn_out_g[l].reshape(1, -1).astype(F32),
                    batch, seq, tb)
        o_b = _mlstm(zb, zf, _lane_row(mlstm_if_bias[l, 0], 0), _lane_row(mlstm_if_bias[l, 1], 0),
                     mlstm_out_g[l].reshape(1, -1).astype(F32), batch, seq, tb)
        o_c = _swa(zb, jnp.tile(attn_q_norm_g[l].astype(F32), AT_HEADS).reshape(1, -1),
                   jnp.tile(attn_k_norm_g[l].astype(F32), 2).reshape(1, -1),
                   jnp.broadcast_to(attn_sinks[l].astype(F32)[:, None], (AT_HEADS, LANES)),
                   bias, batch, seq, 4)
        o_d = _deltanet(zb, zf, dn_conv_w[l].astype(F32), _lane_row(dn_a_log[l], N_HEADS),
                        _lane_row(dn_dt_bias[l], N_HEADS), dn_out_g[l].reshape(1, -1).astype(F32),
                        batch, seq, tb)

        x2 = _merge(o_a, o_b, o_c, o_d, zb, x2, _bf(w_branch[l]), _bf(w_out[l]), 512 if n % 512 == 0 else 256)
        x2 = _mlp(x2, norm_mlp_g[l].reshape(1, d), _bf(w_up[l]), _bf(w_down[l]), tm, 1024)
    return x2.reshape(batch, seq, d)
```

```python
import functools
import math

import numpy as np
import jax
import jax.numpy as jnp
from jax import lax
from jax.experimental import pallas as pl
from jax.experimental.pallas import tpu as pltpu

F32 = jnp.float32
BF16 = jnp.bfloat16

D_MODEL = 1024
EPS = 1e-6
CHUNK = 64
NEG_BIG = -1e30
TINY = 1e-30
LANES = 128
HEAD_W = 128
N_HEADS = 4
BRANCH_W = 512
N_BRANCH = 4
D_FF = 4 * D_MODEL

AT_HEADS = 8
AT_HD = 64
WINDOW = 128
N_BUCKETS = 32
MAX_DISTANCE = 128
CONV_K = 4
ML_DK = 64
DN_DK = 128
CHUNK_SHIFT = 6
AT_HD_SHIFT = 6
LOG2_E = 1.4426950408889634
CONV_TAIL = 16

ZB_HQ, ZB_HI, ZB_HG = 0, 512, 1024
ZB_MQK, ZB_MV, ZB_MO = 1536, 2048, 2560
ZB_AQ, ZB_AKV = 3072, 3584
ZB_DQ, ZB_DK, ZB_DV, ZB_DZ = 4096, 4608, 5120, 5632
ZB_GATE = 6144
ZB_W = 10240
ZF_HF, ZF_S1, ZF_S2 = 0, 512, 640
ZF_W = 768

VMEM_LIMIT = 56 * 1024 * 1024


def _bf(x):
    return x.astype(BF16)


def _mm(a, b):
    return jnp.dot(_bf(a), _bf(b), preferred_element_type=F32)


def _mm_nt(a, b):
    return lax.dot_general(_bf(a), _bf(b), (((1,), (1,)), ((), ())),
                           preferred_element_type=F32)


def _mm_tn(a, b):
    return lax.dot_general(_bf(a), _bf(b), (((0,), (0,)), ((), ())),
                           preferred_element_type=F32)


def _split3(x):
    hi = _bf(x)
    r = x - hi.astype(F32)
    mid = _bf(r)
    lo = _bf(r - mid.astype(F32))
    return hi, mid, lo


def _mm3(sel, x):
    hi, mid, lo = _split3(x)
    return (jnp.dot(sel, hi, preferred_element_type=F32)
            + jnp.dot(sel, mid, preferred_element_type=F32)
            + jnp.dot(sel, lo, preferred_element_type=F32))


def _mm3_nt(sel, x):
    dn = (((1,), (1,)), ((), ()))
    hi, mid, lo = _split3(x)
    return (lax.dot_general(sel, hi, dn, preferred_element_type=F32)
            + lax.dot_general(sel, mid, dn, preferred_element_type=F32)
            + lax.dot_general(sel, lo, dn, preferred_element_type=F32))


def _sigmoid(x):
    return 1.0 / (1.0 + jnp.exp2(x * (-LOG2_E)))


def _silu(x):
    return x * _sigmoid(x)


def _log1pexp_neg_abs(x):
    return jnp.log(1.0 + jnp.exp(-jnp.abs(x)))


def _log_sigmoid(x):
    return jnp.minimum(x, 0.0) - _log1pexp_neg_abs(x)


def _softplus(x):
    return jnp.maximum(x, 0.0) + _log1pexp_neg_abs(x)


def _iota(shape, dim):
    return lax.broadcasted_iota(jnp.int32, shape, dim)


def _chunk_tril(n):
    t = _iota((n, n), 0)
    s = _iota((n, n), 1)
    return ((t >> CHUNK_SHIFT) == (s >> CHUNK_SHIFT)) & (s <= t)


def _head_rms(o):
    return o * lax.rsqrt(jnp.mean(o * o, axis=-1, keepdims=True) + EPS)


def _seg_ones(n, seg_shift):
    a = _iota((n, n), 0) >> seg_shift
    b = _iota((n, n), 1) >> seg_shift
    return jnp.where(a == b, 1.0, 0.0).astype(BF16)


def _inproj_kernel(x_ref, g_ref, wb_ref, wf_ref, zb_ref, zf_ref, h_ref):
    @pl.when(pl.program_id(1) == 0)
    def _():
        x = x_ref[...]
        ms = jnp.mean(x * x, axis=-1, keepdims=True)
        h = _bf((x * lax.rsqrt(ms + EPS)) * g_ref[...])
        h_ref[...] = h
        zf_ref[...] = jnp.dot(h, wf_ref[...], preferred_element_type=F32)

    zb_ref[...] = _bf(jnp.dot(h_ref[...], wb_ref[...], preferred_element_type=F32))


def _inproj(x2, g, wb, wf, tm, tn):
    n, d = x2.shape
    return pl.pallas_call(
        _inproj_kernel,
        grid=(n // tm, ZB_W // tn),
        in_specs=[
            pl.BlockSpec((tm, d), lambda i, j: (i, 0)),
            pl.BlockSpec((1, d), lambda i, j: (0, 0)),
            pl.BlockSpec((d, tn), lambda i, j: (0, j)),
            pl.BlockSpec((d, ZF_W), lambda i, j: (0, 0)),
        ],
        out_specs=[
            pl.BlockSpec((tm, tn), lambda i, j: (i, j)),
            pl.BlockSpec((tm, ZF_W), lambda i, j: (i, 0)),
        ],
        out_shape=[jax.ShapeDtypeStruct((n, ZB_W), BF16),
                   jax.ShapeDtypeStruct((n, ZF_W), F32)],
        scratch_shapes=[pltpu.VMEM((tm, d), BF16)],
        compiler_params=pltpu.CompilerParams(
            dimension_semantics=("arbitrary", "arbitrary"),
            vmem_limit_bytes=VMEM_LIMIT),
        name="inproj",
    )(x2, g, wb, wf)


def _hgrn_kernel(q_ref, v_ref, g_ref, f_ref, lb_ref, og_ref, o_ref, st_ref, *, tb):
    @pl.when(pl.program_id(1) == 0)
    def _():
        st_ref[...] = jnp.zeros(st_ref.shape, F32)

    n_chunks = tb // CHUNK
    z = f_ref[...]
    lb = lb_ref[...]
    sig = _sigmoid(z)
    f = lb + (1.0 - lb) * sig
    logf2 = jnp.log(jnp.maximum(f, TINY)) * LOG2_E
    k = (1.0 - lb) * (1.0 - sig)
    q = _silu(q_ref[...].astype(F32))

    t_sq = _iota((tb, tb), 0)
    r_sq = _iota((tb, tb), 1)
    hi = _bf(logf2)
    lo = _bf(logf2 - hi.astype(F32))

    def seg_sum(mask):
        sel = jnp.where(mask, 1.0, 0.0).astype(BF16)
        return (jnp.dot(sel, hi, preferred_element_type=F32)
                + jnp.dot(sel, lo, preferred_element_type=F32))

    n_lv = CHUNK_SHIFT
    row = _iota((tb, 1), 0)
    qd, kd = [], []
    for lv in range(n_lv):
        upper = ((row >> lv) & 1) == 1
        if lv == 0:
            qd.append(_bf(q * jnp.exp2(jnp.where(upper, logf2, NEG_BIG))))
            kd.append(_bf(jnp.where(upper, 0.0, k)))
        else:
            same_seg = (t_sq >> lv) == (r_sq >> lv)
            prefix = seg_sum(same_seg & (r_sq <= t_sq))
            qd.append(_bf(q * jnp.exp2(jnp.where(upper, prefix, NEG_BIG))))
            yield
            suffix = seg_sum(same_seg & (r_sq > t_sq))
            kd.append(_bf(k * jnp.exp2(jnp.where(upper, NEG_BIG, suffix))))
        yield
    same_chunk = (t_sq >> n_lv) == (r_sq >> n_lv)
    cum2 = seg_sum(same_chunk & (r_sq <= t_sq))
    q_in = _bf(q * jnp.exp2(cum2))
    yield
    k_out = _bf(k * jnp.exp2(seg_sum(same_chunk & (r_sq > t_sq))))
    qk = _bf(q * k)
    ones_b = jnp.ones((HEAD_W, CHUNK), BF16)
    t_i = _iota((CHUNK, CHUNK), 0)
    s_i = _iota((CHUNK, CHUNK), 1)
    block_masks = [(t_i >> (lv + 1)) == (s_i >> (lv + 1)) for lv in range(n_lv)]
    eye = t_i == s_i

    heads = range(N_HEADS)
    cs = [slice(h * HEAD_W, (h + 1) * HEAD_W) for h in heads]
    intra = [[None] * N_HEADS for _ in range(n_chunks)]
    for ci in range(n_chunks):
        rs = slice(ci * CHUNK, (ci + 1) * CHUNK)
        a = [jnp.where(eye, jnp.dot(qk[rs, cs[h]], ones_b, preferred_element_type=F32), 0.0)
             for h in heads]
        for lv in range(n_lv):
            a = [a[h] + jnp.where(block_masks[lv], _mm_nt(qd[lv][rs, cs[h]], kd[lv][rs, cs[h]]), 0.0)
                 for h in heads]
            if lv % 2 == 1:
                yield
        for h in heads:
            intra[ci][h] = _mm(a[h], v_ref[rs, cs[h]])

    st = [st_ref[h] for h in heads]
    for ci in range(n_chunks):
        rs = slice(ci * CHUNK, (ci + 1) * CHUNK)
        for h in heads:
            o = intra[ci][h] + _mm_nt(q_in[rs, cs[h]], st[h])
            total = cum2[(ci + 1) * CHUNK - 1:(ci + 1) * CHUNK, cs[h]]
            st[h] = jnp.exp2(total) * st[h] + _mm_tn(v_ref[rs, cs[h]], k_out[rs, cs[h]])
            o = _head_rms(o) * og_ref[:, cs[h]] * _silu(g_ref[rs, cs[h]].astype(F32))
            o_ref[rs, cs[h]] = _bf(o)
        yield
    for h in heads:
        st_ref[h] = st[h]


def _mlstm_kernel(qk_ref, v_ref, og_pre_ref, s1_ref, s2_ref, b0_ref, b1_ref, gain_ref,
                  o_ref, c_ref, m_ref, *, tb):
    @pl.when(pl.program_id(1) == 0)
    def _():
        c_ref[...] = jnp.zeros(c_ref.shape, F32)
        m_ref[...] = jnp.zeros(m_ref.shape, F32)

    li_all = s1_ref[...] + b0_ref[...]
    lf_all = _log_sigmoid(s2_ref[...] + b1_ref[...])
    tri = jnp.where(_chunk_tril(tb), 1.0, 0.0).astype(BF16)
    cum_all = _mm3(tri, lf_all)
    rr_all = li_all - cum_all
    eye = jnp.where(_iota((16, LANES), 0) == _iota((16, LANES), 1), 1.0, 0.0).astype(BF16)
    causal = _iota((CHUNK, CHUNK), 1) <= _iota((CHUNK, CHUNK), 0)
    lane = _iota((1, LANES), 1)
    e0 = jnp.where(_iota((CHUNK, LANES), 1) == 0, 1.0, 0.0).astype(BF16)

    n_chunks = tb // CHUNK
    pre = [[None] * N_HEADS for _ in range(n_chunks)]
    for ci in range(n_chunks):
        rs = slice(ci * CHUNK, (ci + 1) * CHUNK)
        rr_t = _mm3_nt(eye, rr_all[rs])
        for h in range(N_HEADS):
            pair = h // 2
            half = (lane >> AT_HD_SHIFT) == (h % 2)
            gs = slice(pair * LANES, (pair + 1) * LANES)
            qm = jnp.where(half, qk_ref[rs, gs], jnp.zeros((), BF16))
            ks = qk_ref[rs, 2 * LANES + pair * LANES:2 * LANES + (pair + 1) * LANES].astype(F32) * (ML_DK ** -0.5)
            vs = slice(h * HEAD_W, (h + 1) * HEAD_W)
            v_aug = jnp.concatenate([v_ref[rs, vs], e0], axis=1)
            cum_col = cum_all[rs, h:h + 1]
            li_col = li_all[rs, h:h + 1]
            logd = jnp.where(causal, cum_col + rr_t[h:h + 1, :], NEG_BIG)
            rowmax = jnp.max(logd, axis=-1, keepdims=True)
            p0 = _mm(_mm_nt(qm, ks) * jnp.exp(logd - rowmax), v_aug)
            mref = rowmax[CHUNK - 1:CHUNK, :]
            cum_last = cum_col[CHUNK - 1:CHUNK, :]
            upd = _mm_tn(ks * jnp.exp(cum_last - cum_col + li_col - mref), v_aug)
            pre[ci][h] = (qm, cum_col, rowmax, p0, mref, cum_last, upd)
            if h % 2 == 1:
                yield

    cms = [c_ref[h] for h in range(N_HEADS)]
    ms = [m_ref[h:h + 1, 0:1] for h in range(N_HEADS)]
    for ci in range(n_chunks):
        rs = slice(ci * CHUNK, (ci + 1) * CHUNK)
        for h in range(N_HEADS):
            vs = slice(h * HEAD_W, (h + 1) * HEAD_W)
            qm, cum_col, rowmax, p0, mref, cum_last, upd = pre[ci][h]
            m_old = ms[h]
            m_inter = cum_col + m_old
            m_t = jnp.maximum(m_inter, rowmax)
            nd = jnp.exp(rowmax - m_t) * p0 + jnp.exp(m_inter - m_t) * _mm(qm, cms[h])
            num = nd[:, :HEAD_W]
            den = nd[:, HEAD_W:HEAD_W + 1]
            hval = num / jnp.maximum(jnp.abs(den), jnp.exp(-m_t))
            m_new = m_t[CHUNK - 1:CHUNK, :]
            cms[h] = jnp.exp(cum_last + m_old - m_new) * cms[h] + jnp.exp(mref - m_new) * upd
            ms[h] = m_new
            out = _head_rms(hval) * gain_ref[:, vs] * _sigmoid(og_pre_ref[rs, vs].astype(F32))
            o_ref[rs, vs] = _bf(out)
        yield
    for h in range(N_HEADS):
        c_ref[h] = cms[h]
        m_ref[h:h + 1, :] = jnp.broadcast_to(ms[h], (1, LANES))


def _swa_kernel(q_ref, kvp_ref, kvc_ref, qg_ref, kg_ref, sink_ref, bias_ref, o_ref, *, n_win):
    first_cols = jnp.where(pl.program_id(1) == 0, WINDOW, 0)

    qf = q_ref[...].astype(F32)
    ms_q = _mm(qf * qf, _seg_ones(BRANCH_W, AT_HD_SHIFT)) * (1.0 / AT_HD)
    qn = _bf(qf * lax.rsqrt(ms_q + EPS) * qg_ref[...] * (AT_HD ** -0.5))

    kw = jnp.concatenate([kvp_ref[:, :LANES], kvc_ref[:, :LANES]], axis=0).astype(F32)
    vw = jnp.concatenate([kvp_ref[:, LANES:], kvc_ref[:, LANES:]], axis=0).astype(F32)
    ms_k = _mm(kw * kw, _seg_ones(LANES, AT_HD_SHIFT)) * (1.0 / AT_HD)
    kn = kw * lax.rsqrt(ms_k + EPS) * kg_ref[...]
    lane = _iota((1, LANES), 1)
    halves = [(lane >> AT_HD_SHIFT) == i for i in range(2)]
    k_src = {True: _bf(kn), False: _bf(pltpu.roll(kn, AT_HD, 1))}
    v_roll = pltpu.roll(vw, AT_HD, 1)
    v_src = {(same, i): _bf(jnp.where(halves[i], vw if same else v_roll, 0.0))
             for same in (True, False) for i in range(2)}

    kpos = _iota((1, 2 * WINDOW), 1)
    yield
    for w in range(n_win):
        qs = slice(w * WINDOW, (w + 1) * WINDOW)
        ws = slice(w * WINDOW, (w + 2) * WINDOW)
        for pair in range(AT_HEADS // 2):
            acc = jnp.zeros((WINDOW, LANES), F32)
            for half_i in range(2):
                h = 2 * pair + half_i
                same = half_i == h // (AT_HEADS // 2)
                qh = jnp.where(halves[half_i], qn[qs, pair * LANES:(pair + 1) * LANES], jnp.zeros((), BF16))
                lg = _mm_nt(qh, k_src[same][ws]) + bias_ref[h]
                if w == 0:
                    lg = jnp.where(kpos < first_cols, NEG_BIG, lg)
                sink = sink_ref[h:h + 1, 0:1]
                mx = jnp.maximum(jnp.max(lg, axis=-1, keepdims=True), sink)
                p = jnp.exp(lg - mx)
                denom = jnp.sum(p, axis=-1, keepdims=True) + jnp.exp(sink - mx)
                acc = acc + _mm(p, v_src[(same, half_i)][ws]) / denom
            o_ref[qs, pair * LANES:(pair + 1) * LANES] = _bf(acc)
            yield


def _conv_silu(x_ref, prev_ref, w_ref, idx, has_prev):
    tb = x_ref.shape[0]
    x = x_ref[...].astype(F32)
    prev = prev_ref[CONV_TAIL - 8:, :].astype(F32) * has_prev
    x2 = jnp.concatenate([prev, x], axis=0)
    w = w_ref[:, idx * BRANCH_W:(idx + 1) * BRANCH_W]
    y = x * w[CONV_K - 1:CONV_K, :]
    for k in range(1, CONV_K):
        y = y + x2[8 - k:8 - k + tb, :] * w[CONV_K - 1 - k:CONV_K - k, :]
    return _silu(y)


def _dn_kernel(q_ref, k_ref, v_ref, qp_ref, kp_ref, vp_ref, z_ref, s1_ref, s2_ref, w_ref,
               alog_ref, dt_ref, gain_ref, o_ref, st_ref, *, tb):
    @pl.when(pl.program_id(1) == 0)
    def _():
        st_ref[...] = jnp.zeros(st_ref.shape, F32)

    n_chunks = tb // CHUNK
    has_prev = jnp.where(pl.program_id(1) > 0, 1.0, 0.0)
    q = _conv_silu(q_ref, qp_ref, w_ref, 0, has_prev)
    yield
    k = _conv_silu(k_ref, kp_ref, w_ref, 1, has_prev)
    yield
    v = _conv_silu(v_ref, vp_ref, w_ref, 2, has_prev)
    yield
    seg = _seg_ones(BRANCH_W, 7)
    q = q * lax.rsqrt(_mm(q * q, seg) + EPS) * (DN_DK ** -0.5)
    k = k * lax.rsqrt(_mm(k * k, seg) + EPS)
    yield

    beta_all = _sigmoid(s1_ref[...])
    g_all = -jnp.exp(alog_ref[...]) * _softplus(s2_ref[...] + dt_ref[...])
    incl = _chunk_tril(tb)
    tri = jnp.where(incl, 1.0, 0.0).astype(BF16)
    gam_all = _mm3(tri, g_all)
    eye = jnp.where(_iota((16, LANES), 0) == _iota((16, LANES), 1), 1.0, 0.0).astype(BF16)
    gam_t = _mm3_nt(eye, gam_all)
    tile = 2 * CHUNK
    n_tiles = tb // tile
    t_sq = _iota((tile, tile), 0)
    s_sq = _iota((tile, tile), 1)
    incl_t = _chunk_tril(tile)
    level_masks = []
    for lv in range(CHUNK_SHIFT):
        m = 1 << lv
        level_masks.append(((t_sq >> (lv + 1)) == (s_sq >> (lv + 1)))
                           & ((t_sq & m) != 0) & ((s_sq & m) == 0))

    heads = range(N_HEADS)
    cs = [slice(h * HEAD_W, (h + 1) * HEAD_W) for h in heads]
    cells = [(ti, h) for ti in range(n_tiles) for h in heads]
    rt = {ti: slice(ti * tile, (ti + 1) * tile) for ti in range(n_tiles)}
    gcol = {h: gam_all[:, N_HEADS + h:N_HEADS + h + 1] for h in heads}
    bcol = {h: beta_all[:, N_HEADS + h:N_HEADS + h + 1] for h in heads}
    decay = {(ti, h): jnp.exp(jnp.where(
        incl_t, gcol[h][rt[ti]] - gam_t[N_HEADS + h:N_HEADS + h + 1, rt[ti]], NEG_BIG))
        for ti, h in cells}
    yield
    kb = {c: _bf(k[rt[c[0]], cs[c[1]]]) for c in cells}
    low = {c: bcol[c[1]][rt[c[0]]] * _mm_nt(kb[c], kb[c]) * decay[c] for c in cells}
    yield
    n_mat = {c: -jnp.where(level_masks[0], low[c], 0.0) for c in cells}
    for lm in level_masks[1:]:
        x_m = {}
        for i, c in enumerate(cells):
            l_m = jnp.where(lm, low[c], 0.0)
            x_m[c] = l_m + _mm(l_m, n_mat[c])
            if i % 4 == 3:
                yield
        for i, c in enumerate(cells):
            n_mat[c] = n_mat[c] - x_m[c] - _mm(n_mat[c], x_m[c])
            if i % 4 == 3:
                yield
    sol, attn = {}, {}
    for i, c in enumerate(cells):
        ti, h = c
        bc = bcol[h][rt[ti]]
        rhs = jnp.concatenate([v[rt[ti], cs[h]] * bc,
                               k[rt[ti], cs[h]] * (bc * jnp.exp(gcol[h][rt[ti]]))], axis=1)
        sol[c] = rhs + _mm(n_mat[c], rhs)
        attn[c] = _bf(_mm_nt(q[rt[ti], cs[h]], kb[c]) * decay[c])
        if i % 4 == 3:
            yield
    q_dec = {h: q[:, cs[h]] * jnp.exp(gcol[h]) for h in heads}
    yield

    st = [st_ref[h] for h in heads]
    per_tile = tile // CHUNK
    for ci in range(n_chunks):
        rs = slice(ci * CHUNK, (ci + 1) * CHUNK)
        ti, cj = divmod(ci, per_tile)
        rl = slice(cj * CHUNK, (cj + 1) * CHUNK)
        if cj == 0:
            vnew = {h: [] for h in heads}
        pad = [jnp.zeros((CHUNK, HEAD_W), BF16)] * (per_tile - cj - 1)
        for h in heads:
            c = (ti, h)
            g_last = gcol[h][ci * CHUNK + CHUNK - 1:(ci + 1) * CHUNK, :]
            ws = _mm(jnp.concatenate([sol[c][rl, HEAD_W:], q_dec[h][rs]], axis=0), st[h])
            v_new = sol[c][rl, :HEAD_W] - ws[:CHUNK]
            vnew[h].append(_bf(v_new))
            v_full = jnp.concatenate(vnew[h] + pad, axis=0)
            o = ws[CHUNK:] + jnp.dot(attn[c][rl], v_full, preferred_element_type=F32)
            k_dec = k[rs, cs[h]] * jnp.exp(g_last - gcol[h][rs])
            st[h] = jnp.exp(g_last) * st[h] + _mm_tn(k_dec, v_new)
            out = _head_rms(o) * gain_ref[...] * _silu(z_ref[rs, cs[h]].astype(F32))
            o_ref[rs, cs[h]] = _bf(out)
            if h % 2 == 1:
                yield
    for h in heads:
        st_ref[h] = st[h]


def _round_robin(gens):
    gens = list(gens)
    while gens:
        for g in list(gens):
            try:
                next(g)
            except StopIteration:
                gens.remove(g)


def _mixers_kernel(hq, hv, hg, hf, lb, og_h,
                   mqk, mv, mo, s1, s2, b0, b1, gain_m,
                   aq, kvp, kvc, qg, kg, sinks, bias,
                   dq, dk, dv, dqp, dkp, dvp, dz, conv_w, alog, dt, gain_d,
                   o_a, o_b, o_c, o_d, st_h, c_m, m_m, st_d, *, tb, nb):
    gens = []
    for i in range(nb):
        at = lambda *refs, i=i: [r.at[i] for r in refs]
        gens += [
            _dn_kernel(*at(dq, dk, dv, dqp, dkp, dvp, dz, s1, s2), conv_w, alog, dt, gain_d,
                       o_d.at[i], st_d.at[i], tb=tb),
            _hgrn_kernel(*at(hq, hv, hg, hf), lb, og_h, o_a.at[i], st_h.at[i], tb=tb),
            _mlstm_kernel(*at(mqk, mv, mo, s1, s2), b0, b1, gain_m, o_b.at[i], c_m.at[i], m_m.at[i], tb=tb),
            _swa_kernel(*at(aq, kvp, kvc), qg, kg, sinks, bias, o_c.at[i], n_win=tb // WINDOW),
        ]
    _round_robin(gens)


def _mixers(zb, zf, lb, og_h, b0, b1, gain_m, qg, kg, sinks, bias, conv_w, alog, dt, gain_d,
            batch, seq, tb, nb):
    nt = seq // tb
    blk = lambda c: pl.BlockSpec((nb, tb, BRANCH_W), lambda b, t, c=c: (b, t, c // BRANCH_W))
    sm = lambda c: pl.BlockSpec((nb, tb, LANES), lambda b, t, c=c: (b, t, c // LANES))
    full = lambda shape: pl.BlockSpec(shape, lambda b, t: (0,) * len(shape))
    kvw = 2 * LANES
    n_win = tb // WINDOW
    per = tb // CONV_TAIL
    kv_prev = pl.BlockSpec((nb, WINDOW, kvw),
                           lambda b, t: (b, jnp.maximum(t * n_win - 1, 0), ZB_AKV // kvw))
    dn_prev = lambda c: pl.BlockSpec(
        (nb, CONV_TAIL, BRANCH_W),
        lambda b, t, c=c: (b, jnp.maximum(t * per - 1, 0), c // BRANCH_W))
    out_spec = pl.BlockSpec((nb, tb, BRANCH_W), lambda b, t: (b, t, 0))
    out_shape = jax.ShapeDtypeStruct((batch, seq, BRANCH_W), BF16)
    return pl.pallas_call(
        functools.partial(_mixers_kernel, tb=tb, nb=nb),
        grid=(batch // nb, nt),
        in_specs=[
            blk(ZB_HQ), blk(ZB_HI), blk(ZB_HG), blk(ZF_HF), full((1, BRANCH_W)), full((1, BRANCH_W)),
            blk(ZB_MQK), blk(ZB_MV), blk(ZB_MO), sm(ZF_S1), sm(ZF_S2),
            full((1, LANES)), full((1, LANES)), full((1, BRANCH_W)),
            blk(ZB_AQ), kv_prev, pl.BlockSpec((nb, tb, kvw), lambda b, t: (b, t, ZB_AKV // kvw)),
            full((1, BRANCH_W)), full((1, LANES)), full((AT_HEADS, LANES)),
            full((AT_HEADS, WINDOW, 2 * WINDOW)),
            blk(ZB_DQ), blk(ZB_DK), blk(ZB_DV), dn_prev(ZB_DQ), dn_prev(ZB_DK), dn_prev(ZB_DV),
            blk(ZB_DZ), full((CONV_K, 3 * BRANCH_W)), full((1, LANES)), full((1, LANES)),
            full((1, LANES)),
        ],
        out_specs=[out_spec] * 4,
        out_shape=[out_shape] * 4,
        scratch_shapes=[pltpu.VMEM((nb, N_HEADS, HEAD_W, HEAD_W), F32),
                        pltpu.VMEM((nb, N_HEADS, LANES, 2 * HEAD_W), F32),
                        pltpu.VMEM((nb, 8, LANES), F32),
                        pltpu.VMEM((nb, N_HEADS, DN_DK, HEAD_W), F32)],
        compiler_params=pltpu.CompilerParams(
            dimension_semantics=("arbitrary", "arbitrary"),
            vmem_limit_bytes=VMEM_LIMIT),
        name="mixers",
    )(zb, zb, zb, zf, lb, og_h,
      zb, zb, zb, zf, zf, b0, b1, gain_m,
      zb, zb, zb, qg, kg, sinks, bias,
      zb, zb, zb, zb, zb, zb, zb, conv_w, alog, dt, gain_d)


def _merge_kernel(oa_ref, ob_ref, oc_ref, od_ref, ga_ref, gb_ref, gc_ref, gd_ref,
                  x_ref, wb_ref, wo_ref, y_ref):
    merged = None
    branches = ((oa_ref, ga_ref), (ob_ref, gb_ref), (oc_ref, gc_ref), (od_ref, gd_ref))
    for n, (o_ref, gate_ref) in enumerate(branches):
        proj = jnp.dot(o_ref[...], wb_ref[n], preferred_element_type=F32)
        term = _sigmoid(gate_ref[...].astype(F32)) * proj
        merged = term if merged is None else merged + term
    y_ref[...] = x_ref[...] + jnp.dot(_bf(merged), wo_ref[...], preferred_element_type=F32)


def _merge(oa, ob, oc, od, zb, x2, wbr, wo, tm):
    n = x2.shape[0]
    ob_spec = pl.BlockSpec((tm, BRANCH_W), lambda i: (i, 0))
    gate = lambda k: pl.BlockSpec((tm, D_MODEL), lambda i, k=k: (i, ZB_GATE // D_MODEL + k))
    return pl.pallas_call(
        _merge_kernel,
        grid=(n // tm,),
        in_specs=[ob_spec, ob_spec, ob_spec, ob_spec,
                  gate(0), gate(1), gate(2), gate(3),
                  pl.BlockSpec((tm, D_MODEL), lambda i: (i, 0)),
                  pl.BlockSpec((N_BRANCH, BRANCH_W, D_MODEL), lambda i: (0, 0, 0)),
                  pl.BlockSpec((D_MODEL, D_MODEL), lambda i: (0, 0))],
        out_specs=pl.BlockSpec((tm, D_MODEL), lambda i: (i, 0)),
        out_shape=jax.ShapeDtypeStruct((n, D_MODEL), F32),
        compiler_params=pltpu.CompilerParams(
            dimension_semantics=("arbitrary",), vmem_limit_bytes=VMEM_LIMIT),
        name="merge",
    )(oa, ob, oc, od, zb, zb, zb, zb, x2, wbr, wo)


def _mlp_kernel(x_ref, g_ref, wu_ref, wd_ref, y_ref, h_ref, acc_ref):
    j = pl.program_id(1)

    @pl.when(j == 0)
    def _():
        x = x_ref[...]
        ms = jnp.mean(x * x, axis=-1, keepdims=True)
        h_ref[...] = _bf((x * lax.rsqrt(ms + EPS)) * g_ref[...])
        acc_ref[...] = jnp.zeros(acc_ref.shape, F32)

    up = jnp.dot(h_ref[...], wu_ref[...], preferred_element_type=F32)
    act = jnp.square(jnp.maximum(up, 0.0))
    acc_ref[...] += jnp.dot(_bf(act), wd_ref[...], preferred_element_type=F32)

    @pl.when(j == pl.num_programs(1) - 1)
    def _():
        y_ref[...] = x_ref[...] + acc_ref[...]


def _mlp(x2, g, wu, wd, tm, tf):
    n = x2.shape[0]
    return pl.pallas_call(
        _mlp_kernel,
        grid=(n // tm, D_FF // tf),
        in_specs=[pl.BlockSpec((tm, D_MODEL), lambda i, j: (i, 0)),
                  pl.BlockSpec((1, D_MODEL), lambda i, j: (0, 0)),
                  pl.BlockSpec((D_MODEL, tf), lambda i, j: (0, j)),
                  pl.BlockSpec((tf, D_MODEL), lambda i, j: (j, 0))],
        out_specs=pl.BlockSpec((tm, D_MODEL), lambda i, j: (i, 0)),
        out_shape=jax.ShapeDtypeStruct((n, D_MODEL), F32),
        scratch_shapes=[pltpu.VMEM((tm, D_MODEL), BF16), pltpu.VMEM((tm, D_MODEL), F32)],
        compiler_params=pltpu.CompilerParams(
            dimension_semantics=("arbitrary", "arbitrary"), vmem_limit_bytes=VMEM_LIMIT),
        name="mlp",
    )(x2, g, wu, wd)


def _t5_bucket_table():
    n = np.arange(WINDOW)
    max_exact = N_BUCKETS // 2
    nf = np.maximum(n, max_exact).astype(np.float32)
    large = max_exact + (np.log(nf / np.float32(max_exact)) / np.float32(math.log(MAX_DISTANCE / max_exact))
                         * (N_BUCKETS - max_exact)).astype(np.int32)
    large = np.minimum(large, N_BUCKETS - 1)
    return np.where(n < max_exact, n, large)


def _swa_bias(rel_table):
    per_dist = rel_table.astype(F32)[_t5_bucket_table()].T
    n_heads = per_dist.shape[0]
    span = 3 * WINDOW
    pad_lo = jnp.full((n_heads, WINDOW - 1), NEG_BIG, F32)
    pad_hi = jnp.full((n_heads, span - 2 * WINDOW + 1), NEG_BIG, F32)
    v = jnp.concatenate([pad_lo, per_dist, pad_hi], axis=1)
    hank = jnp.tile(v, (1, WINDOW + 1))[:, :WINDOW * (span + 1)].reshape(n_heads, WINDOW, span + 1)
    return hank[:, :, :2 * WINDOW][:, :, ::-1]


def _lane_row(vals, offset):
    return jnp.zeros((1, LANES), F32).at[0, offset:offset + vals.shape[0]].set(vals.astype(F32))


def _layout_w_in(w):
    sizes = (512, 512, 512, 512, 256, 256, 512, 4, 4, 512, 512, 128, 128, 1536, 4, 4, 512, 4096)
    parts, start = [], 0
    for s in sizes:
        parts.append(w[:, start:start + s])
        start += s
    (hq, hf, hi, hg, mq, mk, mv, mi, mf, mo, aq, ak, av, dqkv, db, da, dz, gate) = parts
    zpad = lambda n: jnp.zeros((w.shape[0], n), w.dtype)
    wb = jnp.concatenate([hq, hi, hg, mq, mk, mv, mo, aq, ak, av, zpad(256), dqkv, dz, gate], axis=1)
    wf = jnp.concatenate([hf, mi, db, zpad(LANES - 8), mf, da, zpad(LANES - 8)], axis=1)
    return _bf(wb), _bf(wf)


def kernel(x, norm_mix_g, w_in, hgrn_lb_table, hgrn_out_g, mlstm_if_bias, mlstm_out_g,
           attn_q_norm_g, attn_k_norm_g, attn_sinks, rel_bias_table, dn_conv_w, dn_a_log,
           dn_dt_bias, dn_out_g, w_branch, w_out, norm_mlp_g, w_up, w_down):
    batch, seq, d = x.shape
    depth = w_in.shape[0]
    n = batch * seq
    tb = 256
    nb = 1
    assert seq % tb == 0 and d == D_MODEL
    tm = 1024 if n % 1024 == 0 else 256

    lb_p = jax.nn.softmax(hgrn_lb_table.astype(F32), axis=0)
    lower_bounds = jnp.cumsum(lb_p, axis=0) - lb_p[0]
    bias = _swa_bias(rel_bias_table)

    x2 = x.reshape(n, d)
    for l in range(depth):
        wb, wf = _layout_w_in(w_in[l])
        zb, zf = _inproj(x2, norm_mix_g[l].reshape(1, d), wb, wf, tm, 2048)

        o_a, o_b, o_c, o_d = _mixers(
            zb.reshape(batch, seq, ZB_W), zf.reshape(batch, seq, ZF_W),
            lower_bounds[l].reshape(1, -1), hgrn_out_g[l].reshape(1, -1).astype(F32),
            _lane_row(mlstm_if_bias[l, 0], 0), _lane_row(mlstm_if_bias[l, 1], 0),
            mlstm_out_g[l].reshape(1, -1).astype(F32),
            jnp.tile(attn_q_norm_g[l].astype(F32), AT_HEADS).reshape(1, -1),
            jnp.tile(attn_k_norm_g[l].astype(F32), 2).reshape(1, -1),
            jnp.broadcast_to(attn_sinks[l].astype(F32)[:, None], (AT_HEADS, LANES)), bias,
            dn_conv_w[l].astype(F32), _lane_row(dn_a_log[l], N_HEADS),
            _lane_row(dn_dt_bias[l], N_HEADS), dn_out_g[l].reshape(1, -1).astype(F32),
            batch, seq, tb, nb)
        o_a, o_b, o_c, o_d = (o.reshape(n, BRANCH_W) for o in (o_a, o_b, o_c, o_d))

        x2 = _merge(o_a, o_b, o_c, o_d, zb, x2, _bf(w_branch[l]), _bf(w_out[l]), 512 if n % 512 == 0 else 256)
        x2 = _mlp(x2, norm_mlp_g[l].reshape(1, d), _bf(w_up[l]), _bf(w_down[l]), tm, 1024)
    return x2.reshape(batch, seq, d)
```

```python
import functools
import math

import numpy as np
import jax
import jax.numpy as jnp
from jax import lax
from jax.experimental import pallas as pl
from jax.experimental.pallas import tpu as pltpu

F32 = jnp.float32
BF16 = jnp.bfloat16

D_MODEL = 1024
EPS = 1e-6
CHUNK = 64
NEG_BIG = -1e30
TINY = 1e-30
LANES = 128
HEAD_W = 128
N_HEADS = 4
BRANCH_W = 512
N_BRANCH = 4
D_FF = 4 * D_MODEL

AT_HEADS = 8
AT_HD = 64
WINDOW = 128
N_BUCKETS = 32
MAX_DISTANCE = 128
CONV_K = 4
ML_DK = 64
DN_DK = 128
CHUNK_SHIFT = 6
AT_HD_SHIFT = 6
LOG2_E = 1.4426950408889634
CONV_TAIL = 16

ZB_HQ, ZB_HI, ZB_HG = 0, 512, 1024
ZB_MQK, ZB_MV, ZB_MO = 1536, 2048, 2560
ZB_AQ, ZB_AKV = 3072, 3584
ZB_DQ, ZB_DK, ZB_DV, ZB_DZ = 4096, 4608, 5120, 5632
ZB_GATE = 6144
ZB_W = 10240
ZF_HF, ZF_S1, ZF_S2 = 0, 512, 640
ZF_W = 768

VMEM_LIMIT = 56 * 1024 * 1024


def _bf(x):
    return x.astype(BF16)


def _mm(a, b):
    return jnp.dot(_bf(a), _bf(b), preferred_element_type=F32)


def _mm_nt(a, b):
    return lax.dot_general(_bf(a), _bf(b), (((1,), (1,)), ((), ())),
                           preferred_element_type=F32)


def _mm_tn(a, b):
    return lax.dot_general(_bf(a), _bf(b), (((0,), (0,)), ((), ())),
                           preferred_element_type=F32)


def _split3(x):
    hi = _bf(x)
    r = x - hi.astype(F32)
    mid = _bf(r)
    lo = _bf(r - mid.astype(F32))
    return hi, mid, lo


def _mm3(sel, x):
    hi, mid, lo = _split3(x)
    return (jnp.dot(sel, hi, preferred_element_type=F32)
            + jnp.dot(sel, mid, preferred_element_type=F32)
            + jnp.dot(sel, lo, preferred_element_type=F32))


def _mm3_nt(sel, x):
    dn = (((1,), (1,)), ((), ()))
    hi, mid, lo = _split3(x)
    return (lax.dot_general(sel, hi, dn, preferred_element_type=F32)
            + lax.dot_general(sel, mid, dn, preferred_element_type=F32)
            + lax.dot_general(sel, lo, dn, preferred_element_type=F32))


def _sigmoid(x):
    return 1.0 / (1.0 + jnp.exp2(x * (-LOG2_E)))


def _silu(x):
    return x * _sigmoid(x)


def _log1pexp_neg_abs(x):
    return jnp.log(1.0 + jnp.exp(-jnp.abs(x)))


def _log_sigmoid(x):
    return jnp.minimum(x, 0.0) - _log1pexp_neg_abs(x)


def _softplus(x):
    return jnp.maximum(x, 0.0) + _log1pexp_neg_abs(x)


def _iota(shape, dim):
    return lax.broadcasted_iota(jnp.int32, shape, dim)


def _chunk_tril(n):
    t = _iota((n, n), 0)
    s = _iota((n, n), 1)
    return ((t >> CHUNK_SHIFT) == (s >> CHUNK_SHIFT)) & (s <= t)


def _head_rms(o):
    return o * lax.rsqrt(jnp.mean(o * o, axis=-1, keepdims=True) + EPS)


def _seg_ones(n, seg_shift):
    a = _iota((n, n), 0) >> seg_shift
    b = _iota((n, n), 1) >> seg_shift
    return jnp.where(a == b, 1.0, 0.0).astype(BF16)


def _inproj_kernel(x_ref, g_ref, wb_ref, wf_ref, zb_ref, zf_ref, h_ref):
    @pl.when(pl.program_id(1) == 0)
    def _():
        x = x_ref[...]
        ms = jnp.mean(x * x, axis=-1, keepdims=True)
        h = _bf((x * lax.rsqrt(ms + EPS)) * g_ref[...])
        h_ref[...] = h
        zf_ref[...] = jnp.dot(h, wf_ref[...], preferred_element_type=F32)

    zb_ref[...] = _bf(jnp.dot(h_ref[...], wb_ref[...], preferred_element_type=F32))


def _inproj(x2, g, wb, wf, tm, tn):
    n, d = x2.shape
    return pl.pallas_call(
        _inproj_kernel,
        grid=(n // tm, ZB_W // tn),
        in_specs=[
            pl.BlockSpec((tm, d), lambda i, j: (i, 0)),
            pl.BlockSpec((1, d), lambda i, j: (0, 0)),
            pl.BlockSpec((d, tn), lambda i, j: (0, j)),
            pl.BlockSpec((d, ZF_W), lambda i, j: (0, 0)),
        ],
        out_specs=[
            pl.BlockSpec((tm, tn), lambda i, j: (i, j)),
            pl.BlockSpec((tm, ZF_W), lambda i, j: (i, 0)),
        ],
        out_shape=[jax.ShapeDtypeStruct((n, ZB_W), BF16),
                   jax.ShapeDtypeStruct((n, ZF_W), F32)],
        scratch_shapes=[pltpu.VMEM((tm, d), BF16)],
        compiler_params=pltpu.CompilerParams(
            dimension_semantics=("arbitrary", "arbitrary"),
            vmem_limit_bytes=VMEM_LIMIT),
        name="inproj",
    )(x2, g, wb, wf)


def _hgrn_kernel(q_ref, v_ref, g_ref, f_ref, lb_ref, og_ref, o_ref, st_ref, *, tb):
    @pl.when(pl.program_id(1) == 0)
    def _():
        st_ref[...] = jnp.zeros(st_ref.shape, F32)

    n_chunks = tb // CHUNK
    z = f_ref[...]
    lb = lb_ref[...]
    sig = _sigmoid(z)
    f = lb + (1.0 - lb) * sig
    logf2 = jnp.log(jnp.maximum(f, TINY)) * LOG2_E
    k = (1.0 - lb) * (1.0 - sig)
    q = _silu(q_ref[...].astype(F32))

    t_sq = _iota((tb, tb), 0)
    r_sq = _iota((tb, tb), 1)
    hi = _bf(logf2)
    lo = _bf(logf2 - hi.astype(F32))

    def seg_sum(mask):
        sel = jnp.where(mask, 1.0, 0.0).astype(BF16)
        return (jnp.dot(sel, hi, preferred_element_type=F32)
                + jnp.dot(sel, lo, preferred_element_type=F32))

    n_lv = CHUNK_SHIFT
    row = _iota((tb, 1), 0)
    qd, kd = [], []
    for lv in range(n_lv):
        upper = ((row >> lv) & 1) == 1
        if lv == 0:
            qd.append(_bf(q * jnp.exp2(jnp.where(upper, logf2, NEG_BIG))))
            kd.append(_bf(jnp.where(upper, 0.0, k)))
        else:
            same_seg = (t_sq >> lv) == (r_sq >> lv)
            prefix = seg_sum(same_seg & (r_sq <= t_sq))
            qd.append(_bf(q * jnp.exp2(jnp.where(upper, prefix, NEG_BIG))))
            yield
            suffix = seg_sum(same_seg & (r_sq > t_sq))
            kd.append(_bf(k * jnp.exp2(jnp.where(upper, NEG_BIG, suffix))))
        yield
    same_chunk = (t_sq >> n_lv) == (r_sq >> n_lv)
    cum2 = seg_sum(same_chunk & (r_sq <= t_sq))
    q_in = _bf(q * jnp.exp2(cum2))
    yield
    k_out = _bf(k * jnp.exp2(seg_sum(same_chunk & (r_sq > t_sq))))
    qk = _bf(q * k)
    ones_b = jnp.ones((HEAD_W, CHUNK), BF16)
    t_i = _iota((CHUNK, CHUNK), 0)
    s_i = _iota((CHUNK, CHUNK), 1)
    block_masks = [(t_i >> (lv + 1)) == (s_i >> (lv + 1)) for lv in range(n_lv)]
    eye = t_i == s_i

    heads = range(N_HEADS)
    cs = [slice(h * HEAD_W, (h + 1) * HEAD_W) for h in heads]
    st = [st_ref[h] for h in heads]
    for ci in range(n_chunks):
        rs = slice(ci * CHUNK, (ci + 1) * CHUNK)
        a = [jnp.where(eye, jnp.dot(qk[rs, cs[h]], ones_b, preferred_element_type=F32), 0.0)
             for h in heads]
        for lv in range(n_lv):
            a = [a[h] + jnp.where(block_masks[lv], _mm_nt(qd[lv][rs, cs[h]], kd[lv][rs, cs[h]]), 0.0)
                 for h in heads]
            if lv % 2 == 1:
                yield
        for h in heads:
            o = _mm(a[h], v_ref[rs, cs[h]]) + _mm_nt(q_in[rs, cs[h]], st[h])
            total = cum2[(ci + 1) * CHUNK - 1:(ci + 1) * CHUNK, cs[h]]
            st[h] = jnp.exp2(total) * st[h] + _mm_tn(v_ref[rs, cs[h]], k_out[rs, cs[h]])
            o = _head_rms(o) * og_ref[:, cs[h]] * _silu(g_ref[rs, cs[h]].astype(F32))
            o_ref[rs, cs[h]] = _bf(o)
        yield
    for h in heads:
        st_ref[h] = st[h]


def _mlstm_kernel(qk_ref, v_ref, og_pre_ref, s1_ref, s2_ref, b0_ref, b1_ref, gain_ref,
                  o_ref, c_ref, m_ref, *, tb):
    @pl.when(pl.program_id(1) == 0)
    def _():
        c_ref[...] = jnp.zeros(c_ref.shape, F32)
        m_ref[...] = jnp.zeros(m_ref.shape, F32)

    li_all = s1_ref[...] + b0_ref[...]
    lf_all = _log_sigmoid(s2_ref[...] + b1_ref[...])
    tri = jnp.where(_chunk_tril(tb), 1.0, 0.0).astype(BF16)
    cum_all = _mm3(tri, lf_all)
    rr_all = li_all - cum_all
    eye = jnp.where(_iota((16, LANES), 0) == _iota((16, LANES), 1), 1.0, 0.0).astype(BF16)
    causal = _iota((CHUNK, CHUNK), 1) <= _iota((CHUNK, CHUNK), 0)
    lane = _iota((1, LANES), 1)
    ones_b = jnp.ones((CHUNK, LANES), BF16)
    ones_sq = jnp.ones((LANES, LANES), BF16)
    cum_bs = [jnp.broadcast_to(cum_all[:, h:h + 1], (tb, LANES)) for h in range(N_HEADS)]
    li_bs = [jnp.broadcast_to(li_all[:, h:h + 1], (tb, LANES)) for h in range(N_HEADS)]

    n_chunks = tb // CHUNK
    pre = [[None] * N_HEADS for _ in range(n_chunks)]
    cms = [c_ref[h] for h in range(N_HEADS)]
    ms = [m_ref[h:h + 1, :] for h in range(N_HEADS)]

    def recur(ci):
        rs = slice(ci * CHUNK, (ci + 1) * CHUNK)
        for h in range(N_HEADS):
            vs = slice(h * HEAD_W, (h + 1) * HEAD_W)
            qm, cum_b, rowmax_b, p0, mref_b, cum_last_b, upd = pre[ci][h]
            m_old = ms[h]
            m_inter = cum_b + m_old
            m_t = jnp.maximum(m_inter, rowmax_b)
            e2 = jnp.exp(rowmax_b - m_t)
            wi = jnp.exp(m_inter - m_t)
            qc = _mm(qm, cms[h])
            num = e2 * p0[:, :HEAD_W] + wi * qc[:, :HEAD_W]
            den = e2 * p0[:, HEAD_W:] + wi * qc[:, HEAD_W:]
            hval = num / jnp.maximum(jnp.abs(den), jnp.exp(-m_t))
            m_new = m_t[CHUNK - 1:CHUNK, :]
            dec = jnp.exp(cum_last_b + m_old - m_new)
            sc = jnp.exp(mref_b - m_new)
            cms[h] = (jnp.concatenate([dec, dec], axis=1) * cms[h]
                      + jnp.concatenate([sc, sc], axis=1) * upd)
            ms[h] = m_new
            mean_sq = _mm(hval * hval, ones_sq) * (1.0 / HEAD_W)
            out = (hval * lax.rsqrt(mean_sq + EPS) * gain_ref[:, vs]
                   * _sigmoid(og_pre_ref[rs, vs].astype(F32)))
            o_ref[rs, vs] = _bf(out)

    for ci in range(n_chunks):
        rs = slice(ci * CHUNK, (ci + 1) * CHUNK)
        rr_t = _mm3_nt(eye, rr_all[rs])
        for h in range(N_HEADS):
            pair = h // 2
            half = (lane >> AT_HD_SHIFT) == (h % 2)
            gs = slice(pair * LANES, (pair + 1) * LANES)
            qm = jnp.where(half, qk_ref[rs, gs], jnp.zeros((), BF16))
            ks = qk_ref[rs, 2 * LANES + pair * LANES:2 * LANES + (pair + 1) * LANES].astype(F32) * (ML_DK ** -0.5)
            vs = slice(h * HEAD_W, (h + 1) * HEAD_W)
            v_aug = jnp.concatenate([v_ref[rs, vs], ones_b], axis=1)
            cum_b = cum_bs[h][rs]
            logd = jnp.where(causal, cum_b[:, :CHUNK] + rr_t[h:h + 1, :], NEG_BIG)
            rowmax_b = jnp.broadcast_to(jnp.max(logd, axis=-1, keepdims=True), (CHUNK, LANES))
            p0 = _mm(_mm_nt(qm, ks) * jnp.exp(logd - rowmax_b[:, :CHUNK]), v_aug)
            mref_b = rowmax_b[CHUNK - 1:CHUNK, :]
            cum_last_b = cum_b[CHUNK - 1:CHUNK, :]
            upd = _mm_tn(ks * jnp.exp(cum_last_b - cum_b + li_bs[h][rs] - mref_b), v_aug)
            pre[ci][h] = (qm, cum_b, rowmax_b, p0, mref_b, cum_last_b, upd)
            if h % 2 == 1:
                yield
        if ci >= 1:
            recur(ci - 1)
            yield
    recur(n_chunks - 1)
    yield
    for h in range(N_HEADS):
        c_ref[h] = cms[h]
        m_ref[h:h + 1, :] = ms[h]


def _swa_kernel(q_ref, kvp_ref, kvc_ref, qg_ref, kg_ref, sink_ref, bias_ref, o_ref, *, n_win):
    first_cols = jnp.where(pl.program_id(1) == 0, WINDOW, 0)

    qf = q_ref[...].astype(F32)
    ms_q = _mm(qf * qf, _seg_ones(BRANCH_W, AT_HD_SHIFT)) * (1.0 / AT_HD)
    qn = _bf(qf * lax.rsqrt(ms_q + EPS) * qg_ref[...] * (AT_HD ** -0.5))

    kw = jnp.concatenate([kvp_ref[:, :LANES], kvc_ref[:, :LANES]], axis=0).astype(F32)
    vw = jnp.concatenate([kvp_ref[:, LANES:], kvc_ref[:, LANES:]], axis=0).astype(F32)
    ms_k = _mm(kw * kw, _seg_ones(LANES, AT_HD_SHIFT)) * (1.0 / AT_HD)
    kn = kw * lax.rsqrt(ms_k + EPS) * kg_ref[...]
    lane = _iota((1, LANES), 1)
    halves = [(lane >> AT_HD_SHIFT) == i for i in range(2)]
    k_src = {True: _bf(kn), False: _bf(pltpu.roll(kn, AT_HD, 1))}
    v_roll = pltpu.roll(vw, AT_HD, 1)
    v_src = {(same, i): _bf(jnp.where(halves[i], vw if same else v_roll, 0.0))
             for same in (True, False) for i in range(2)}

    kpos = _iota((1, 2 * WINDOW), 1)
    yield
    for w in range(n_win):
        qs = slice(w * WINDOW, (w + 1) * WINDOW)
        ws = slice(w * WINDOW, (w + 2) * WINDOW)
        for pair in range(AT_HEADS // 2):
            acc = jnp.zeros((WINDOW, LANES), F32)
            for half_i in range(2):
                h = 2 * pair + half_i
                same = half_i == h // (AT_HEADS // 2)
                qh = jnp.where(halves[half_i], qn[qs, pair * LANES:(pair + 1) * LANES], jnp.zeros((), BF16))
                lg = _mm_nt(qh, k_src[same][ws]) + bias_ref[h]
                if w == 0:
                    lg = jnp.where(kpos < first_cols, NEG_BIG, lg)
                sink = sink_ref[h:h + 1, 0:1]
                mx = jnp.maximum(jnp.max(lg, axis=-1, keepdims=True), sink)
                p = jnp.exp(lg - mx)
                denom = jnp.sum(p, axis=-1, keepdims=True) + jnp.exp(sink - mx)
                acc = acc + _mm(p, v_src[(same, half_i)][ws]) / denom
            o_ref[qs, pair * LANES:(pair + 1) * LANES] = _bf(acc)
            yield


def _conv_silu(x_ref, prev_ref, w_ref, idx, has_prev):
    tb = x_ref.shape[0]
    x = x_ref[...].astype(F32)
    prev = prev_ref[CONV_TAIL - 8:, :].astype(F32) * has_prev
    x2 = jnp.concatenate([prev, x], axis=0)
    w = w_ref[:, idx * BRANCH_W:(idx + 1) * BRANCH_W]
    y = x * w[CONV_K - 1:CONV_K, :]
    for k in range(1, CONV_K):
        y = y + x2[8 - k:8 - k + tb, :] * w[CONV_K - 1 - k:CONV_K - k, :]
    return _silu(y)


def _dn_kernel(q_ref, k_ref, v_ref, qp_ref, kp_ref, vp_ref, z_ref, s1_ref, s2_ref, w_ref,
               alog_ref, dt_ref, gain_ref, o_ref, st_ref, *, tb):
    @pl.when(pl.program_id(1) == 0)
    def _():
        st_ref[...] = jnp.zeros(st_ref.shape, F32)

    n_chunks = tb // CHUNK
    has_prev = jnp.where(pl.program_id(1) > 0, 1.0, 0.0)
    q = _conv_silu(q_ref, qp_ref, w_ref, 0, has_prev)
    yield
    k = _conv_silu(k_ref, kp_ref, w_ref, 1, has_prev)
    yield
    v = _conv_silu(v_ref, vp_ref, w_ref, 2, has_prev)
    yield
    seg = _seg_ones(BRANCH_W, 7)
    q = q * lax.rsqrt(_mm(q * q, seg) + EPS) * (DN_DK ** -0.5)
    k = k * lax.rsqrt(_mm(k * k, seg) + EPS)
    yield

    beta_all = _sigmoid(s1_ref[...])
    g_all = -jnp.exp(alog_ref[...]) * _softplus(s2_ref[...] + dt_ref[...])
    incl = _chunk_tril(tb)
    tri = jnp.where(incl, 1.0, 0.0).astype(BF16)
    gam_all = _mm3(tri, g_all)
    eye = jnp.where(_iota((16, LANES), 0) == _iota((16, LANES), 1), 1.0, 0.0).astype(BF16)
    gam_t = _mm3_nt(eye, gam_all)
    tile = 2 * CHUNK
    n_tiles = tb // tile
    t_sq = _iota((tile, tile), 0)
    s_sq = _iota((tile, tile), 1)
    incl_t = _chunk_tril(tile)
    level_masks = []
    for lv in range(CHUNK_SHIFT):
        m = 1 << lv
        level_masks.append(((t_sq >> (lv + 1)) == (s_sq >> (lv + 1)))
                           & ((t_sq & m) != 0) & ((s_sq & m) == 0))

    heads = range(N_HEADS)
    cs = [slice(h * HEAD_W, (h + 1) * HEAD_W) for h in heads]
    gcol = {h: gam_all[:, N_HEADS + h:N_HEADS + h + 1] for h in heads}
    bcol = {h: beta_all[:, N_HEADS + h:N_HEADS + h + 1] for h in heads}
    q_dec = {h: q[:, cs[h]] * jnp.exp(gcol[h]) for h in heads}
    sol, attn = {}, {}
    yield

    def prep(ti):
        rt = slice(ti * tile, (ti + 1) * tile)
        decay = {h: jnp.exp(jnp.where(
            incl_t, gcol[h][rt] - gam_t[N_HEADS + h:N_HEADS + h + 1, rt], NEG_BIG)) for h in heads}
        kb = {h: _bf(k[rt, cs[h]]) for h in heads}
        low = {h: bcol[h][rt] * _mm_nt(kb[h], kb[h]) * decay[h] for h in heads}
        yield
        n_mat = {h: -jnp.where(level_masks[0], low[h], 0.0) for h in heads}
        for lm in level_masks[1:]:
            x_m = {}
            for h in heads:
                l_m = jnp.where(lm, low[h], 0.0)
                x_m[h] = l_m + _mm(l_m, n_mat[h])
            yield
            for h in heads:
                n_mat[h] = n_mat[h] - x_m[h] - _mm(n_mat[h], x_m[h])
            yield
        for h in heads:
            bc = bcol[h][rt]
            rhs = jnp.concatenate([v[rt, cs[h]] * bc,
                                   k[rt, cs[h]] * (bc * jnp.exp(gcol[h][rt]))], axis=1)
            sol[ti, h] = rhs + _mm(n_mat[h], rhs)
            attn[ti, h] = _bf(_mm_nt(q[rt, cs[h]], kb[h]) * decay[h])
        yield

    st = [st_ref[h] for h in heads]
    per_tile = tile // CHUNK
    vnew = {}

    def recur(ci):
        rs = slice(ci * CHUNK, (ci + 1) * CHUNK)
        ti, cj = divmod(ci, per_tile)
        rl = slice(cj * CHUNK, (cj + 1) * CHUNK)
        pad = [jnp.zeros((CHUNK, HEAD_W), BF16)] * (per_tile - cj - 1)
        for h in heads:
            if cj == 0:
                vnew[h] = []
            g_last = gcol[h][ci * CHUNK + CHUNK - 1:(ci + 1) * CHUNK, :]
            ws = _mm(jnp.concatenate([sol[ti, h][rl, HEAD_W:], q_dec[h][rs]], axis=0), st[h])
            v_new = sol[ti, h][rl, :HEAD_W] - ws[:CHUNK]
            vnew[h].append(_bf(v_new))
            v_full = jnp.concatenate(vnew[h] + pad, axis=0)
            o = ws[CHUNK:] + jnp.dot(attn[ti, h][rl], v_full, preferred_element_type=F32)
            k_dec = k[rs, cs[h]] * jnp.exp(g_last - gcol[h][rs])
            st[h] = jnp.exp(g_last) * st[h] + _mm_tn(k_dec, v_new)
            out = _head_rms(o) * gain_ref[...] * _silu(z_ref[rs, cs[h]].astype(F32))
            o_ref[rs, cs[h]] = _bf(out)

    todo = []
    for ti in range(n_tiles):
        for si, _ in enumerate(prep(ti)):
            yield
            if todo and si % 4 == 3:
                recur(todo.pop(0))
                yield
        while todo:
            recur(todo.pop(0))
            yield
        todo = list(range(ti * per_tile, (ti + 1) * per_tile))
    while todo:
        recur(todo.pop(0))
        yield
    for h in heads:
        st_ref[h] = st[h]


def _round_robin(gens):
    gens = list(gens)
    while gens:
        for g in list(gens):
            try:
                next(g)
            except StopIteration:
                gens.remove(g)


def _mixers_kernel(hq, hv, hg, hf, lb, og_h,
                   mqk, mv, mo, s1, s2, b0, b1, gain_m,
                   aq, kvp, kvc, qg, kg, sinks, bias,
                   dq, dk, dv, dqp, dkp, dvp, dz, conv_w, alog, dt, gain_d,
                   o_a, o_b, o_c, o_d, st_h, c_m, m_m, st_d, *, tb, nb):
    gens = []
    for i in range(nb):
        at = lambda *refs, i=i: [r.at[i] for r in refs]
        gens += [
            _dn_kernel(*at(dq, dk, dv, dqp, dkp, dvp, dz, s1, s2), conv_w, alog, dt, gain_d,
                       o_d.at[i], st_d.at[i], tb=tb),
            _hgrn_kernel(*at(hq, hv, hg, hf), lb, og_h, o_a.at[i], st_h.at[i], tb=tb),
            _mlstm_kernel(*at(mqk, mv, mo, s1, s2), b0, b1, gain_m, o_b.at[i], c_m.at[i], m_m.at[i], tb=tb),
            _swa_kernel(*at(aq, kvp, kvc), qg, kg, sinks, bias, o_c.at[i], n_win=tb // WINDOW),
        ]
    _round_robin(gens)


def _mixers(zb, zf, lb, og_h, b0, b1, gain_m, qg, kg, sinks, bias, conv_w, alog, dt, gain_d,
            batch, seq, tb, nb):
    nt = seq // tb
    blk = lambda c: pl.BlockSpec((nb, tb, BRANCH_W), lambda b, t, c=c: (b, t, c // BRANCH_W))
    sm = lambda c: pl.BlockSpec((nb, tb, LANES), lambda b, t, c=c: (b, t, c // LANES))
    full = lambda shape: pl.BlockSpec(shape, lambda b, t: (0,) * len(shape))
    kvw = 2 * LANES
    n_win = tb // WINDOW
    per = tb // CONV_TAIL
    kv_prev = pl.BlockSpec((nb, WINDOW, kvw),
                           lambda b, t: (b, jnp.maximum(t * n_win - 1, 0), ZB_AKV // kvw))
    dn_prev = lambda c: pl.BlockSpec(
        (nb, CONV_TAIL, BRANCH_W),
        lambda b, t, c=c: (b, jnp.maximum(t * per - 1, 0), c // BRANCH_W))
    out_spec = pl.BlockSpec((nb, tb, BRANCH_W), lambda b, t: (b, t, 0))
    out_shape = jax.ShapeDtypeStruct((batch, seq, BRANCH_W), BF16)
    return pl.pallas_call(
        functools.partial(_mixers_kernel, tb=tb, nb=nb),
        grid=(batch // nb, nt),
        in_specs=[
            blk(ZB_HQ), blk(ZB_HI), blk(ZB_HG), blk(ZF_HF), full((1, BRANCH_W)), full((1, BRANCH_W)),
            blk(ZB_MQK), blk(ZB_MV), blk(ZB_MO), sm(ZF_S1), sm(ZF_S2),
            full((1, LANES)), full((1, LANES)), full((1, BRANCH_W)),
            blk(ZB_AQ), kv_prev, pl.BlockSpec((nb, tb, kvw), lambda b, t: (b, t, ZB_AKV // kvw)),
            full((1, BRANCH_W)), full((1, LANES)), full((AT_HEADS, LANES)),
            full((AT_HEADS, WINDOW, 2 * WINDOW)),
            blk(ZB_DQ), blk(ZB_DK), blk(ZB_DV), dn_prev(ZB_DQ), dn_prev(ZB_DK), dn_prev(ZB_DV),
            blk(ZB_DZ), full((CONV_K, 3 * BRANCH_W)), full((1, LANES)), full((1, LANES)),
            full((1, LANES)),
        ],
        out_specs=[out_spec] * 4,
        out_shape=[out_shape] * 4,
        scratch_shapes=[pltpu.VMEM((nb, N_HEADS, HEAD_W, HEAD_W), F32),
                        pltpu.VMEM((nb, N_HEADS, LANES, 2 * HEAD_W), F32),
                        pltpu.VMEM((nb, 8, LANES), F32),
                        pltpu.VMEM((nb, N_HEADS, DN_DK, HEAD_W), F32)],
        compiler_params=pltpu.CompilerParams(
            dimension_semantics=("arbitrary", "arbitrary"),
            vmem_limit_bytes=VMEM_LIMIT),
        name="mixers",
    )(zb, zb, zb, zf, lb, og_h,
      zb, zb, zb, zf, zf, b0, b1, gain_m,
      zb, zb, zb, qg, kg, sinks, bias,
      zb, zb, zb, zb, zb, zb, zb, conv_w, alog, dt, gain_d)


def _merge_kernel(oa_ref, ob_ref, oc_ref, od_ref, ga_ref, gb_ref, gc_ref, gd_ref,
                  x_ref, wb_ref, wo_ref, y_ref):
    merged = None
    branches = ((oa_ref, ga_ref), (ob_ref, gb_ref), (oc_ref, gc_ref), (od_ref, gd_ref))
    for n, (o_ref, gate_ref) in enumerate(branches):
        proj = jnp.dot(o_ref[...], wb_ref[n], preferred_element_type=F32)
        term = _sigmoid(gate_ref[...].astype(F32)) * proj
        merged = term if merged is None else merged + term
    y_ref[...] = x_ref[...] + jnp.dot(_bf(merged), wo_ref[...], preferred_element_type=F32)


def _merge(oa, ob, oc, od, zb, x2, wbr, wo, tm):
    n = x2.shape[0]
    ob_spec = pl.BlockSpec((tm, BRANCH_W), lambda i: (i, 0))
    gate = lambda k: pl.BlockSpec((tm, D_MODEL), lambda i, k=k: (i, ZB_GATE // D_MODEL + k))
    return pl.pallas_call(
        _merge_kernel,
        grid=(n // tm,),
        in_specs=[ob_spec, ob_spec, ob_spec, ob_spec,
                  gate(0), gate(1), gate(2), gate(3),
                  pl.BlockSpec((tm, D_MODEL), lambda i: (i, 0)),
                  pl.BlockSpec((N_BRANCH, BRANCH_W, D_MODEL), lambda i: (0, 0, 0)),
                  pl.BlockSpec((D_MODEL, D_MODEL), lambda i: (0, 0))],
        out_specs=pl.BlockSpec((tm, D_MODEL), lambda i: (i, 0)),
        out_shape=jax.ShapeDtypeStruct((n, D_MODEL), F32),
        compiler_params=pltpu.CompilerParams(
            dimension_semantics=("arbitrary",), vmem_limit_bytes=VMEM_LIMIT),
        name="merge",
    )(oa, ob, oc, od, zb, zb, zb, zb, x2, wbr, wo)


def _mlp_kernel(x_ref, g_ref, wu_ref, wd_ref, y_ref, h_ref, acc_ref):
    j = pl.program_id(1)

    @pl.when(j == 0)
    def _():
        x = x_ref[...]
        ms = jnp.mean(x * x, axis=-1, keepdims=True)
        h_ref[...] = _bf((x * lax.rsqrt(ms + EPS)) * g_ref[...])
        acc_ref[...] = jnp.zeros(acc_ref.shape, F32)

    up = jnp.dot(h_ref[...], wu_ref[...], preferred_element_type=F32)
    act = jnp.square(jnp.maximum(up, 0.0))
    acc_ref[...] += jnp.dot(_bf(act), wd_ref[...], preferred_element_type=F32)

    @pl.when(j == pl.num_programs(1) - 1)
    def _():
        y_ref[...] = x_ref[...] + acc_ref[...]


def _mlp(x2, g, wu, wd, tm, tf):
    n = x2.shape[0]
    return pl.pallas_call(
        _mlp_kernel,
        grid=(n // tm, D_FF // tf),
        in_specs=[pl.BlockSpec((tm, D_MODEL), lambda i, j: (i, 0)),
                  pl.BlockSpec((1, D_MODEL), lambda i, j: (0, 0)),
                  pl.BlockSpec((D_MODEL, tf), lambda i, j: (0, j)),
                  pl.BlockSpec((tf, D_MODEL), lambda i, j: (j, 0))],
        out_specs=pl.BlockSpec((tm, D_MODEL), lambda i, j: (i, 0)),
        out_shape=jax.ShapeDtypeStruct((n, D_MODEL), F32),
        scratch_shapes=[pltpu.VMEM((tm, D_MODEL), BF16), pltpu.VMEM((tm, D_MODEL), F32)],
        compiler_params=pltpu.CompilerParams(
            dimension_semantics=("arbitrary", "arbitrary"), vmem_limit_bytes=VMEM_LIMIT),
        name="mlp",
    )(x2, g, wu, wd)


def _t5_bucket_table():
    n = np.arange(WINDOW)
    max_exact = N_BUCKETS // 2
    nf = np.maximum(n, max_exact).astype(np.float32)
    large = max_exact + (np.log(nf / np.float32(max_exact)) / np.float32(math.log(MAX_DISTANCE / max_exact))
                         * (N_BUCKETS - max_exact)).astype(np.int32)
    large = np.minimum(large, N_BUCKETS - 1)
    return np.where(n < max_exact, n, large)


def _swa_bias(rel_table):
    per_dist = rel_table.astype(F32)[_t5_bucket_table()].T
    n_heads = per_dist.shape[0]
    span = 3 * WINDOW
    pad_lo = jnp.full((n_heads, WINDOW - 1), NEG_BIG, F32)
    pad_hi = jnp.full((n_heads, span - 2 * WINDOW + 1), NEG_BIG, F32)
    v = jnp.concatenate([pad_lo, per_dist, pad_hi], axis=1)
    hank = jnp.tile(v, (1, WINDOW + 1))[:, :WINDOW * (span + 1)].reshape(n_heads, WINDOW, span + 1)
    return hank[:, :, :2 * WINDOW][:, :, ::-1]


def _lane_row(vals, offset):
    return jnp.zeros((1, LANES), F32).at[0, offset:offset + vals.shape[0]].set(vals.astype(F32))


def _layout_w_in(w):
    w = _bf(w)
    sizes = (512, 512, 512, 512, 256, 256, 512, 4, 4, 512, 512, 128, 128, 1536, 4, 4, 512, 4096)
    parts, start = [], 0
    for s in sizes:
        parts.append(w[:, start:start + s])
        start += s
    (hq, hf, hi, hg, mq, mk, mv, mi, mf, mo, aq, ak, av, dqkv, db, da, dz, gate) = parts
    zpad = lambda n: jnp.zeros((w.shape[0], n), w.dtype)
    wb = jnp.concatenate([hq, hi, hg, mq, mk, mv, mo, aq, ak, av, zpad(256), dqkv, dz, gate], axis=1)
    wf = jnp.concatenate([hf, mi, db, zpad(LANES - 8), mf, da, zpad(LANES - 8)], axis=1)
    return wb, wf


def kernel(x, norm_mix_g, w_in, hgrn_lb_table, hgrn_out_g, mlstm_if_bias, mlstm_out_g,
           attn_q_norm_g, attn_k_norm_g, attn_sinks, rel_bias_table, dn_conv_w, dn_a_log,
           dn_dt_bias, dn_out_g, w_branch, w_out, norm_mlp_g, w_up, w_down):
    batch, seq, d = x.shape
    depth = w_in.shape[0]
    n = batch * seq
    tb = 256
    nb = 1
    assert seq % tb == 0 and d == D_MODEL
    tm = 1024 if n % 1024 == 0 else 256

    lb_p = jax.nn.softmax(hgrn_lb_table.astype(F32), axis=0)
    lower_bounds = jnp.cumsum(lb_p, axis=0) - lb_p[0]
    bias = _swa_bias(rel_bias_table)

    x2 = x.reshape(n, d)
    for l in range(depth):
        wb, wf = _layout_w_in(w_in[l])
        zb, zf = _inproj(x2, norm_mix_g[l].reshape(1, d), wb, wf, tm, 2560)

        o_a, o_b, o_c, o_d = _mixers(
            zb.reshape(batch, seq, ZB_W), zf.reshape(batch, seq, ZF_W),
            lower_bounds[l].reshape(1, -1), hgrn_out_g[l].reshape(1, -1).astype(F32),
            _lane_row(mlstm_if_bias[l, 0], 0), _lane_row(mlstm_if_bias[l, 1], 0),
            mlstm_out_g[l].reshape(1, -1).astype(F32),
            jnp.tile(attn_q_norm_g[l].astype(F32), AT_HEADS).reshape(1, -1),
            jnp.tile(attn_k_norm_g[l].astype(F32), 2).reshape(1, -1),
            jnp.broadcast_to(attn_sinks[l].astype(F32)[:, None], (AT_HEADS, LANES)), bias,
            dn_conv_w[l].astype(F32), _lane_row(dn_a_log[l], N_HEADS),
            _lane_row(dn_dt_bias[l], N_HEADS), dn_out_g[l].reshape(1, -1).astype(F32),
            batch, seq, tb, nb)
        o_a, o_b, o_c, o_d = (o.reshape(n, BRANCH_W) for o in (o_a, o_b, o_c, o_d))

        x2 = _merge(o_a, o_b, o_c, o_d, zb, x2, _bf(w_branch[l]), _bf(w_out[l]), 512 if n % 512 == 0 else 256)
        x2 = _mlp(x2, norm_mlp_g[l].reshape(1, d), _bf(w_up[l]), _bf(w_down[l]), tm, 2048)
    return x2.reshape(batch, seq, d)
```

```python
import functools
import math

import numpy as np
import jax
import jax.numpy as jnp
from jax import lax
from jax.experimental import pallas as pl
from jax.experimental.pallas import tpu as pltpu

F32 = jnp.float32
BF16 = jnp.bfloat16

D_MODEL = 1024
EPS = 1e-6
CHUNK = 64
NEG_BIG = -1e30
TINY = 1e-30
LANES = 128
HEAD_W = 128
N_HEADS = 4
BRANCH_W = 512
N_BRANCH = 4
D_FF = 4 * D_MODEL

AT_HEADS = 8
AT_HD = 64
WINDOW = 128
N_BUCKETS = 32
MAX_DISTANCE = 128
CONV_K = 4
ML_DK = 64
DN_DK = 128
CHUNK_SHIFT = 6
AT_HD_SHIFT = 6
LOG2_E = 1.4426950408889634
CONV_TAIL = 16

ZB_HQ, ZB_HI, ZB_HG = 0, 512, 1024
ZB_MQK, ZB_MV, ZB_MO = 1536, 2048, 2560
ZB_AQ, ZB_AKV = 3072, 3584
ZB_DQ, ZB_DK, ZB_DV, ZB_DZ = 4096, 4608, 5120, 5632
ZB_GATE = 6144
ZB_W = 10240
ZF_HF, ZF_S1, ZF_S2 = 0, 512, 640
ZF_W = 768

VMEM_LIMIT = 56 * 1024 * 1024


def _bf(x):
    return x.astype(BF16)


def _mm(a, b):
    return jnp.dot(_bf(a), _bf(b), preferred_element_type=F32)


def _mm_nt(a, b):
    return lax.dot_general(_bf(a), _bf(b), (((1,), (1,)), ((), ())),
                           preferred_element_type=F32)


def _mm_tn(a, b):
    return lax.dot_general(_bf(a), _bf(b), (((0,), (0,)), ((), ())),
                           preferred_element_type=F32)


def _split3(x):
    hi = _bf(x)
    r = x - hi.astype(F32)
    mid = _bf(r)
    lo = _bf(r - mid.astype(F32))
    return hi, mid, lo


def _mm3(sel, x):
    hi, mid, lo = _split3(x)
    return (jnp.dot(sel, hi, preferred_element_type=F32)
            + jnp.dot(sel, mid, preferred_element_type=F32)
            + jnp.dot(sel, lo, preferred_element_type=F32))


def _mm3_nt(sel, x):
    dn = (((1,), (1,)), ((), ()))
    hi, mid, lo = _split3(x)
    return (lax.dot_general(sel, hi, dn, preferred_element_type=F32)
            + lax.dot_general(sel, mid, dn, preferred_element_type=F32)
            + lax.dot_general(sel, lo, dn, preferred_element_type=F32))


def _sigmoid(x):
    return 1.0 / (1.0 + jnp.exp2(x * (-LOG2_E)))


def _silu(x):
    return x * _sigmoid(x)


def _log1pexp_neg_abs(x):
    return jnp.log(1.0 + jnp.exp(-jnp.abs(x)))


def _log_sigmoid(x):
    return jnp.minimum(x, 0.0) - _log1pexp_neg_abs(x)


def _softplus(x):
    return jnp.maximum(x, 0.0) + _log1pexp_neg_abs(x)


def _iota(shape, dim):
    return lax.broadcasted_iota(jnp.int32, shape, dim)


def _chunk_tril(n):
    t = _iota((n, n), 0)
    s = _iota((n, n), 1)
    return ((t >> CHUNK_SHIFT) == (s >> CHUNK_SHIFT)) & (s <= t)


def _head_rms(o):
    return o * lax.rsqrt(jnp.mean(o * o, axis=-1, keepdims=True) + EPS)


CHUNK_TRIL = 2 * (CHUNK_SHIFT - 1)


def _mixer_constants(tb):
    t = np.arange(tb)[:, None]
    r = np.arange(tb)[None, :]
    segm = []
    for lv in range(1, CHUNK_SHIFT + 1):
        same = (t >> lv) == (r >> lv)
        segm += [same & (r <= t), same & (r > t)]
    lane = np.arange(BRANCH_W)
    block = lambda n, w: (lane[:n, None] // w) == (lane[None, :n] // w)
    as_bf16 = lambda m: jnp.asarray(np.asarray(m, np.float32), BF16)
    return (as_bf16(np.stack(segm)),
            as_bf16(np.eye(16, LANES)),
            as_bf16(block(BRANCH_W, AT_HD)),
            as_bf16(block(LANES, AT_HD)),
            as_bf16(block(BRANCH_W, HEAD_W)))


def _inproj_kernel(x_ref, g_ref, wb_ref, wf_ref, zb_ref, zf_ref, h_ref):
    @pl.when(pl.program_id(1) == 0)
    def _():
        x = x_ref[...]
        ms = jnp.mean(x * x, axis=-1, keepdims=True)
        h = _bf((x * lax.rsqrt(ms + EPS)) * g_ref[...])
        h_ref[...] = h
        zf_ref[...] = jnp.dot(h, wf_ref[...], preferred_element_type=F32)

    zb_ref[...] = _bf(jnp.dot(h_ref[...], wb_ref[...], preferred_element_type=F32))


def _inproj(x2, g, wb, wf, tm, tn):
    n, d = x2.shape
    return pl.pallas_call(
        _inproj_kernel,
        grid=(n // tm, ZB_W // tn),
        in_specs=[
            pl.BlockSpec((tm, d), lambda i, j: (i, 0)),
            pl.BlockSpec((1, d), lambda i, j: (0, 0)),
            pl.BlockSpec((d, tn), lambda i, j: (0, j)),
            pl.BlockSpec((d, ZF_W), lambda i, j: (0, 0)),
        ],
        out_specs=[
            pl.BlockSpec((tm, tn), lambda i, j: (i, j)),
            pl.BlockSpec((tm, ZF_W), lambda i, j: (i, 0)),
        ],
        out_shape=[jax.ShapeDtypeStruct((n, ZB_W), BF16),
                   jax.ShapeDtypeStruct((n, ZF_W), F32)],
        scratch_shapes=[pltpu.VMEM((tm, d), BF16)],
        compiler_params=pltpu.CompilerParams(
            dimension_semantics=("arbitrary", "arbitrary"),
            vmem_limit_bytes=VMEM_LIMIT),
        name="inproj",
    )(x2, g, wb, wf)


def _hgrn_kernel(t_idx, q_ref, v_ref, g_ref, f_ref, lb_ref, og_ref, segm_ref, o_ref, st_ref, *, tb):
    @pl.when(t_idx == 0)
    def _():
        st_ref[...] = jnp.zeros(st_ref.shape, F32)

    n_chunks = tb // CHUNK
    z = f_ref[...]
    lb = lb_ref[...]
    sig = _sigmoid(z)
    f = lb + (1.0 - lb) * sig
    logf2 = jnp.log(jnp.maximum(f, TINY)) * LOG2_E
    k = (1.0 - lb) * (1.0 - sig)
    q = _silu(q_ref[...].astype(F32))

    hi = _bf(logf2)
    lo = _bf(logf2 - hi.astype(F32))

    def seg_sum(idx):
        sel = segm_ref[idx]
        return (jnp.dot(sel, hi, preferred_element_type=F32)
                + jnp.dot(sel, lo, preferred_element_type=F32))

    n_lv = CHUNK_SHIFT
    row = _iota((tb, 1), 0)
    qd, kd = [], []
    for lv in range(n_lv):
        upper = ((row >> lv) & 1) == 1
        if lv == 0:
            qd.append(_bf(q * jnp.exp2(jnp.where(upper, logf2, NEG_BIG))))
            kd.append(_bf(jnp.where(upper, 0.0, k)))
        else:
            prefix = seg_sum(2 * (lv - 1))
            qd.append(_bf(q * jnp.exp2(jnp.where(upper, prefix, NEG_BIG))))
            yield
            suffix = seg_sum(2 * (lv - 1) + 1)
            kd.append(_bf(k * jnp.exp2(jnp.where(upper, NEG_BIG, suffix))))
        yield
    cum2 = seg_sum(2 * (n_lv - 1))
    q_in = _bf(q * jnp.exp2(cum2))
    yield
    k_out = _bf(k * jnp.exp2(seg_sum(2 * (n_lv - 1) + 1)))
    qk = _bf(q * k)
    ones_b = jnp.ones((HEAD_W, CHUNK), BF16)
    t_i = _iota((CHUNK, CHUNK), 0)
    s_i = _iota((CHUNK, CHUNK), 1)
    block_masks = [(t_i >> (lv + 1)) == (s_i >> (lv + 1)) for lv in range(n_lv)]
    eye = t_i == s_i

    heads = range(N_HEADS)
    cs = [slice(h * HEAD_W, (h + 1) * HEAD_W) for h in heads]
    rs = [slice(ci * CHUNK, (ci + 1) * CHUNK) for ci in range(n_chunks)]
    cells = [(ci, h) for ci in range(n_chunks) for h in heads]
    a = {(ci, h): jnp.where(eye, jnp.dot(qk[rs[ci], cs[h]], ones_b, preferred_element_type=F32), 0.0)
         for ci, h in cells}
    yield
    for lv in range(n_lv):
        a = {(ci, h): a[ci, h] + jnp.where(
            block_masks[lv], _mm_nt(qd[lv][rs[ci], cs[h]], kd[lv][rs[ci], cs[h]]), 0.0)
            for ci, h in cells}
        yield
    intra = {(ci, h): _mm(a[ci, h], v_ref[rs[ci], cs[h]]) for ci, h in cells}
    upd = {(ci, h): _mm_tn(v_ref[rs[ci], cs[h]], k_out[rs[ci], cs[h]]) for ci, h in cells}
    yield

    st = [st_ref[h] for h in heads]
    for ci in range(n_chunks):
        o = [intra[ci, h] + _mm_nt(q_in[rs[ci], cs[h]], st[h]) for h in heads]
        total = [cum2[(ci + 1) * CHUNK - 1:(ci + 1) * CHUNK, cs[h]] for h in heads]
        st = [jnp.exp2(total[h]) * st[h] + upd[ci, h] for h in heads]
        rms = [_head_rms(o[h]) for h in heads]
        for h in heads:
            out = rms[h] * og_ref[:, cs[h]] * _silu(g_ref[rs[ci], cs[h]].astype(F32))
            o_ref[rs[ci], cs[h]] = _bf(out)
        yield
    for h in heads:
        st_ref[h] = st[h]


def _mlstm_kernel(t_idx, qk_ref, v_ref, og_pre_ref, s1_ref, s2_ref, b0_ref, b1_ref, gain_ref,
                  segm_ref, eye_ref, o_ref, c_ref, m_ref, *, tb):
    @pl.when(t_idx == 0)
    def _():
        c_ref[...] = jnp.zeros(c_ref.shape, F32)
        m_ref[...] = jnp.zeros(m_ref.shape, F32)

    li_all = s1_ref[...] + b0_ref[...]
    lf_all = _log_sigmoid(s2_ref[...] + b1_ref[...])
    cum_all = _mm3(segm_ref[CHUNK_TRIL], lf_all)
    rr_all = li_all - cum_all
    eye = eye_ref[...]
    causal = _iota((CHUNK, CHUNK), 1) <= _iota((CHUNK, CHUNK), 0)
    lane = _iota((1, LANES), 1)
    ones_b = jnp.ones((CHUNK, LANES), BF16)
    ones_sq = jnp.ones((LANES, LANES), BF16)
    cum_bs = [jnp.broadcast_to(cum_all[:, h:h + 1], (tb, LANES)) for h in range(N_HEADS)]
    li_bs = [jnp.broadcast_to(li_all[:, h:h + 1], (tb, LANES)) for h in range(N_HEADS)]

    n_chunks = tb // CHUNK
    heads = range(N_HEADS)
    cells = [(ci, h) for ci in range(n_chunks) for h in heads]
    rs = [slice(ci * CHUNK, (ci + 1) * CHUNK) for ci in range(n_chunks)]
    vs = [slice(h * HEAD_W, (h + 1) * HEAD_W) for h in heads]
    halves = [(lane >> AT_HD_SHIFT) == i for i in range(2)]
    rr_t = [_mm3_nt(eye, rr_all[rs[ci]]) for ci in range(n_chunks)]
    yield
    qm, ks, v_aug, cum_b = {}, {}, {}, {}
    for ci, h in cells:
        gs = slice((h // 2) * LANES, (h // 2 + 1) * LANES)
        qm[ci, h] = jnp.where(halves[h % 2], qk_ref[rs[ci], gs], jnp.zeros((), BF16))
        ks[ci, h] = qk_ref[rs[ci], 2 * LANES + gs.start:2 * LANES + gs.stop].astype(F32) * (ML_DK ** -0.5)
        v_aug[ci, h] = jnp.concatenate([v_ref[rs[ci], vs[h]], ones_b], axis=1)
        cum_b[ci, h] = cum_bs[h][rs[ci]]
    logd = {c: jnp.where(causal, cum_b[c][:, :CHUNK] + rr_t[c[0]][c[1]:c[1] + 1, :], NEG_BIG)
            for c in cells}
    rowmax_b = {c: jnp.broadcast_to(jnp.max(logd[c], axis=-1, keepdims=True), (CHUNK, LANES))
                for c in cells}
    yield
    qk = {c: _mm_nt(qm[c], ks[c]) for c in cells}
    yield
    s0 = {c: _bf(qk[c] * jnp.exp(logd[c] - rowmax_b[c][:, :CHUNK])) for c in cells}
    yield
    p0 = {c: jnp.dot(s0[c], v_aug[c], preferred_element_type=F32) for c in cells}
    yield
    mref_b = {c: rowmax_b[c][CHUNK - 1:CHUNK, :] for c in cells}
    cum_last_b = {c: cum_b[c][CHUNK - 1:CHUNK, :] for c in cells}
    kw = {c: _bf(ks[c] * jnp.exp(cum_last_b[c] - cum_b[c] + li_bs[c[1]][rs[c[0]]] - mref_b[c]))
          for c in cells}
    yield
    upd = {c: _mm_tn(kw[c], v_aug[c]) for c in cells}
    yield

    cms = [c_ref[h] for h in heads]
    ms = [m_ref[h:h + 1, :] for h in heads]
    for ci in range(n_chunks):
        m_inter = [cum_b[ci, h] + ms[h] for h in heads]
        m_t = [jnp.maximum(m_inter[h], rowmax_b[ci, h]) for h in heads]
        qc = [_mm(qm[ci, h], cms[h]) for h in heads]
        e2 = [jnp.exp(rowmax_b[ci, h] - m_t[h]) for h in heads]
        wi = [jnp.exp(m_inter[h] - m_t[h]) for h in heads]
        m_new = [m_t[h][CHUNK - 1:CHUNK, :] for h in heads]
        dec = [jnp.exp(cum_last_b[ci, h] + ms[h] - m_new[h]) for h in heads]
        sc = [jnp.exp(mref_b[ci, h] - m_new[h]) for h in heads]
        cms = [jnp.concatenate([dec[h], dec[h]], axis=1) * cms[h]
               + jnp.concatenate([sc[h], sc[h]], axis=1) * upd[ci, h] for h in heads]
        ms = m_new
        yield
        num = [e2[h] * p0[ci, h][:, :HEAD_W] + wi[h] * qc[h][:, :HEAD_W] for h in heads]
        den = [e2[h] * p0[ci, h][:, HEAD_W:] + wi[h] * qc[h][:, HEAD_W:] for h in heads]
        hval = [num[h] / jnp.maximum(jnp.abs(den[h]), jnp.exp(-m_t[h])) for h in heads]
        mean_sq = [_mm(hval[h] * hval[h], ones_sq) * (1.0 / HEAD_W) for h in heads]
        for h in heads:
            out = (hval[h] * lax.rsqrt(mean_sq[h] + EPS) * gain_ref[:, vs[h]]
                   * _sigmoid(og_pre_ref[rs[ci], vs[h]].astype(F32)))
            o_ref[rs[ci], vs[h]] = _bf(out)
        yield
    for h in heads:
        c_ref[h] = cms[h]
        m_ref[h:h + 1, :] = ms[h]


def _swa_kernel(t_idx, q_ref, kvp_ref, kvc_ref, qg_ref, kg_ref, sink_ref, bias_ref, segq_ref,
                segk_ref, o_ref, *, n_win):
    first_cols = jnp.where(t_idx == 0, WINDOW, 0)

    qf = q_ref[...].astype(F32)
    ms_q = _mm(qf * qf, segq_ref[...]) * (1.0 / AT_HD)
    qn = _bf(qf * lax.rsqrt(ms_q + EPS) * qg_ref[...] * (AT_HD ** -0.5))

    kw = jnp.concatenate([kvp_ref[:, :LANES], kvc_ref[:, :LANES]], axis=0).astype(F32)
    vw = jnp.concatenate([kvp_ref[:, LANES:], kvc_ref[:, LANES:]], axis=0).astype(F32)
    ms_k = _mm(kw * kw, segk_ref[...]) * (1.0 / AT_HD)
    kn = kw * lax.rsqrt(ms_k + EPS) * kg_ref[...]
    lane = _iota((1, LANES), 1)
    halves = [(lane >> AT_HD_SHIFT) == i for i in range(2)]
    k_src = {True: _bf(kn), False: _bf(pltpu.roll(kn, AT_HD, 1))}
    v_roll = pltpu.roll(vw, AT_HD, 1)
    v_src = {(same, i): _bf(jnp.where(halves[i], vw if same else v_roll, 0.0))
             for same in (True, False) for i in range(2)}

    kpos = _iota((1, 2 * WINDOW), 1)
    yield
    heads = range(AT_HEADS)
    same = [(h % 2) == h // (AT_HEADS // 2) for h in heads]
    sink = [sink_ref[h:h + 1, 0:1] for h in heads]
    cells = [(w, h) for w in range(n_win) for h in heads]
    qs = [slice(w * WINDOW, (w + 1) * WINDOW) for w in range(n_win)]
    ws = [slice(w * WINDOW, (w + 2) * WINDOW) for w in range(n_win)]
    qh = {(w, h): jnp.where(halves[h % 2], qn[qs[w], (h // 2) * LANES:(h // 2 + 1) * LANES],
                            jnp.zeros((), BF16)) for w, h in cells}
    lg = {(w, h): _mm_nt(qh[w, h], k_src[same[h]][ws[w]]) + bias_ref[h] for w, h in cells}
    for h in heads:
        lg[0, h] = jnp.where(kpos < first_cols, NEG_BIG, lg[0, h])
    yield
    mx = {(w, h): jnp.maximum(jnp.max(lg[w, h], axis=-1, keepdims=True), sink[h]) for w, h in cells}
    yield
    p = {c: jnp.exp(lg[c] - mx[c]) for c in cells}
    yield
    denom = {(w, h): jnp.sum(p[w, h], axis=-1, keepdims=True) + jnp.exp(sink[h] - mx[w, h])
             for w, h in cells}
    pv = {(w, h): _mm(p[w, h], v_src[(same[h], h % 2)][ws[w]]) for w, h in cells}
    yield
    for w in range(n_win):
        for pair in range(AT_HEADS // 2):
            acc = (pv[w, 2 * pair] / denom[w, 2 * pair]
                   + pv[w, 2 * pair + 1] / denom[w, 2 * pair + 1])
            o_ref[qs[w], pair * LANES:(pair + 1) * LANES] = _bf(acc)
    yield


def _conv_silu(x_ref, prev_ref, w_ref, idx, has_prev):
    tb = x_ref.shape[0]
    x = x_ref[...].astype(F32)
    prev = prev_ref[CONV_TAIL - 8:, :].astype(F32) * has_prev
    x2 = jnp.concatenate([prev, x], axis=0)
    w = w_ref[:, idx * BRANCH_W:(idx + 1) * BRANCH_W]
    y = x * w[CONV_K - 1:CONV_K, :]
    for k in range(1, CONV_K):
        y = y + x2[8 - k:8 - k + tb, :] * w[CONV_K - 1 - k:CONV_K - k, :]
    return _silu(y)


def _dn_kernel(t_idx, q_ref, k_ref, v_ref, qp_ref, kp_ref, vp_ref, z_ref, s1_ref, s2_ref, w_ref,
               alog_ref, dt_ref, gain_ref, segm_ref, eye_ref, segh_ref, o_ref, st_ref, *, tb):
    @pl.when(t_idx == 0)
    def _():
        st_ref[...] = jnp.zeros(st_ref.shape, F32)

    n_chunks = tb // CHUNK
    has_prev = jnp.where(t_idx > 0, 1.0, 0.0)
    q = _conv_silu(q_ref, qp_ref, w_ref, 0, has_prev)
    yield
    k = _conv_silu(k_ref, kp_ref, w_ref, 1, has_prev)
    yield
    v = _conv_silu(v_ref, vp_ref, w_ref, 2, has_prev)
    yield
    seg = segh_ref[...]
    q = q * lax.rsqrt(_mm(q * q, seg) + EPS) * (DN_DK ** -0.5)
    k = k * lax.rsqrt(_mm(k * k, seg) + EPS)
    yield

    beta_all = _sigmoid(s1_ref[...])
    g_all = -jnp.exp(alog_ref[...]) * _softplus(s2_ref[...] + dt_ref[...])
    gam_all = _mm3(segm_ref[CHUNK_TRIL], g_all)
    gam_t = _mm3_nt(eye_ref[...], gam_all)
    tile = 2 * CHUNK
    n_tiles = tb // tile
    t_sq = _iota((tile, tile), 0)
    s_sq = _iota((tile, tile), 1)
    incl_t = _chunk_tril(tile)
    level_masks = []
    for lv in range(CHUNK_SHIFT):
        m = 1 << lv
        level_masks.append(((t_sq >> (lv + 1)) == (s_sq >> (lv + 1)))
                           & ((t_sq & m) != 0) & ((s_sq & m) == 0))

    heads = range(N_HEADS)
    cs = [slice(h * HEAD_W, (h + 1) * HEAD_W) for h in heads]
    gcol = {h: gam_all[:, N_HEADS + h:N_HEADS + h + 1] for h in heads}
    bcol = {h: beta_all[:, N_HEADS + h:N_HEADS + h + 1] for h in heads}
    q_dec = {h: q[:, cs[h]] * jnp.exp(gcol[h]) for h in heads}
    sol, attn = {}, {}
    yield

    def prep(ti):
        rt = slice(ti * tile, (ti + 1) * tile)
        decay = {h: jnp.exp(jnp.where(
            incl_t, gcol[h][rt] - gam_t[N_HEADS + h:N_HEADS + h + 1, rt], NEG_BIG)) for h in heads}
        kb = {h: _bf(k[rt, cs[h]]) for h in heads}
        low = {h: bcol[h][rt] * _mm_nt(kb[h], kb[h]) * decay[h] for h in heads}
        yield
        n_mat = {h: -jnp.where(level_masks[0], low[h], 0.0) for h in heads}
        for lm in level_masks[1:]:
            x_m = {}
            for h in heads:
                l_m = jnp.where(lm, low[h], 0.0)
                x_m[h] = l_m + _mm(l_m, n_mat[h])
            yield
            for h in heads:
                n_mat[h] = n_mat[h] - x_m[h] - _mm(n_mat[h], x_m[h])
            yield
        for h in heads:
            bc = bcol[h][rt]
            rhs = jnp.concatenate([v[rt, cs[h]] * bc,
                                   k[rt, cs[h]] * (bc * jnp.exp(gcol[h][rt]))], axis=1)
            sol[ti, h] = rhs + _mm(n_mat[h], rhs)
            attn[ti, h] = _bf(_mm_nt(q[rt, cs[h]], kb[h]) * decay[h])
        yield

    st = [st_ref[h] for h in heads]
    per_tile = tile // CHUNK
    vnew = {}

    def recur(ci):
        rs = slice(ci * CHUNK, (ci + 1) * CHUNK)
        ti, cj = divmod(ci, per_tile)
        rl = slice(cj * CHUNK, (cj + 1) * CHUNK)
        pad = [jnp.zeros((CHUNK, HEAD_W), BF16)] * (per_tile - cj - 1)
        if cj == 0:
            for h in heads:
                vnew[h] = []
        g_last = [gcol[h][ci * CHUNK + CHUNK - 1:(ci + 1) * CHUNK, :] for h in heads]
        k_dec = [_bf(k[rs, cs[h]] * jnp.exp(g_last[h] - gcol[h][rs])) for h in heads]
        ws = [_mm(jnp.concatenate([sol[ti, h][rl, HEAD_W:], q_dec[h][rs]], axis=0), st[h])
              for h in heads]
        v_new = [_bf(sol[ti, h][rl, :HEAD_W] - ws[h][:CHUNK]) for h in heads]
        for h in heads:
            vnew[h].append(v_new[h])
        o = [ws[h][CHUNK:] + jnp.dot(attn[ti, h][rl], jnp.concatenate(vnew[h] + pad, axis=0),
                                      preferred_element_type=F32) for h in heads]
        d_st = [_mm_tn(k_dec[h], v_new[h]) for h in heads]
        for h in heads:
            st[h] = jnp.exp(g_last[h]) * st[h] + d_st[h]
        rms = [_head_rms(o[h]) for h in heads]
        for h in heads:
            out = rms[h] * gain_ref[...] * _silu(z_ref[rs, cs[h]].astype(F32))
            o_ref[rs, cs[h]] = _bf(out)

    for _ in zip(*[prep(ti) for ti in range(n_tiles)]):
        yield
    for ci in range(n_chunks):
        recur(ci)
        yield
    for h in heads:
        st_ref[h] = st[h]


def _round_robin(gens):
    gens = list(gens)
    while gens:
        for g in list(gens):
            try:
                next(g)
            except StopIteration:
                gens.remove(g)


def _merge_body(o_scr, gate_refs, x_ref, wb_ref, wo_ref, y_ref):
    outs = [o_scr[:, n * BRANCH_W:(n + 1) * BRANCH_W] for n in range(N_BRANCH)]
    yield
    proj = [jnp.dot(outs[n], wb_ref[n], preferred_element_type=F32) for n in range(N_BRANCH)]
    yield
    gates = [_sigmoid(gate_refs[n][...].astype(F32)) for n in range(N_BRANCH)]
    yield
    merged = (gates[0] * proj[0] + gates[1] * proj[1]) + (gates[2] * proj[2] + gates[3] * proj[3])
    yield
    y_ref[...] = x_ref[...] + jnp.dot(_bf(merged), wo_ref[...], preferred_element_type=F32)
    yield


def _mixers_kernel(hq, hv, hg, hf, lb, og_h,
                   mqk, mv, mo, s1, s2, b0, b1, gain_m,
                   aq, kvp, kvc, qg, kg, sinks, bias,
                   dq, dk, dv, dqp, dkp, dvp, dz, conv_w, alog, dt, gain_d,
                   segm, eye16, seg64q, seg64k, seg128,
                   g0, g1, g2, g3, x_prev, wbr, wo,
                   y_ref, o_scr, st_h, c_m, m_m, st_d, *, tb, nt, total):
    s = pl.program_id(0)
    t_idx = lax.rem(jnp.minimum(s, total - 1), nt)

    @pl.when(s == 0)
    def _():
        o_scr[...] = jnp.zeros(o_scr.shape, BF16)

    at = lambda *refs: [r.at[0] for r in refs]
    o_a, o_b, o_c, o_d = (o_scr.at[:, n * BRANCH_W:(n + 1) * BRANCH_W] for n in range(N_BRANCH))
    _round_robin([
        _merge_body(o_scr, at(g0, g1, g2, g3), x_prev.at[0], wbr, wo, y_ref.at[0]),
        _dn_kernel(t_idx, *at(dq, dk, dv, dqp, dkp, dvp, dz, s1, s2), conv_w, alog, dt, gain_d,
                   segm, eye16, seg128, o_d, st_d, tb=tb),
        _hgrn_kernel(t_idx, *at(hq, hv, hg, hf), lb, og_h, segm, o_a, st_h, tb=tb),
        _mlstm_kernel(t_idx, *at(mqk, mv, mo, s1, s2), b0, b1, gain_m, segm, eye16,
                      o_b, c_m, m_m, tb=tb),
        _swa_kernel(t_idx, *at(aq, kvp, kvc), qg, kg, sinks, bias, seg64q, seg64k, o_c,
                    n_win=tb // WINDOW),
    ])


def _mixers(zb, zf, x3, lb, og_h, b0, b1, gain_m, qg, kg, sinks, bias, conv_w, alog, dt, gain_d,
            wbr, wo, batch, seq, tb):
    nt = seq // tb
    total = batch * nt
    cur = lambda s: jnp.minimum(s, total - 1)
    prv = lambda s: jnp.maximum(s - 1, 0)
    bt = lambda i: (i // nt, i % nt)
    blk = lambda c: pl.BlockSpec((1, tb, BRANCH_W), lambda s, c=c: (*bt(cur(s)), c // BRANCH_W))
    sm = lambda c: pl.BlockSpec((1, tb, LANES), lambda s, c=c: (*bt(cur(s)), c // LANES))
    full = lambda shape: pl.BlockSpec(shape, lambda s: (0,) * len(shape))
    kvw = 2 * LANES
    n_win = tb // WINDOW
    per = tb // CONV_TAIL
    kv_prev = pl.BlockSpec(
        (1, WINDOW, kvw),
        lambda s: (cur(s) // nt, jnp.maximum((cur(s) % nt) * n_win - 1, 0), ZB_AKV // kvw))
    dn_prev = lambda c: pl.BlockSpec(
        (1, CONV_TAIL, BRANCH_W),
        lambda s, c=c: (cur(s) // nt, jnp.maximum((cur(s) % nt) * per - 1, 0), c // BRANCH_W))
    gate = lambda k: pl.BlockSpec((1, tb, D_MODEL), lambda s, k=k: (*bt(prv(s)), ZB_GATE // D_MODEL + k))
    x_spec = pl.BlockSpec((1, tb, D_MODEL), lambda s: (*bt(prv(s)), 0))
    consts = _mixer_constants(tb)
    return pl.pallas_call(
        functools.partial(_mixers_kernel, tb=tb, nt=nt, total=total),
        grid=(total + 1,),
        in_specs=[
            blk(ZB_HQ), blk(ZB_HI), blk(ZB_HG), blk(ZF_HF), full((1, BRANCH_W)), full((1, BRANCH_W)),
            blk(ZB_MQK), blk(ZB_MV), blk(ZB_MO), sm(ZF_S1), sm(ZF_S2),
            full((1, LANES)), full((1, LANES)), full((1, BRANCH_W)),
            blk(ZB_AQ), kv_prev, pl.BlockSpec((1, tb, kvw), lambda s: (*bt(cur(s)), ZB_AKV // kvw)),
            full((1, BRANCH_W)), full((1, LANES)), full((AT_HEADS, LANES)),
            full((AT_HEADS, WINDOW, 2 * WINDOW)),
            blk(ZB_DQ), blk(ZB_DK), blk(ZB_DV), dn_prev(ZB_DQ), dn_prev(ZB_DK), dn_prev(ZB_DV),
            blk(ZB_DZ), full((CONV_K, 3 * BRANCH_W)), full((1, LANES)), full((1, LANES)),
            full((1, LANES)),
        ] + [full(c.shape) for c in consts] + [
            gate(0), gate(1), gate(2), gate(3), x_spec,
            full((N_BRANCH, BRANCH_W, D_MODEL)), full((D_MODEL, D_MODEL)),
        ],
        out_specs=x_spec,
        out_shape=jax.ShapeDtypeStruct((batch, seq, D_MODEL), F32),
        scratch_shapes=[pltpu.VMEM((tb, N_BRANCH * BRANCH_W), BF16),
                        pltpu.VMEM((N_HEADS, HEAD_W, HEAD_W), F32),
                        pltpu.VMEM((N_HEADS, LANES, 2 * HEAD_W), F32),
                        pltpu.VMEM((8, LANES), F32),
                        pltpu.VMEM((N_HEADS, DN_DK, HEAD_W), F32)],
        compiler_params=pltpu.CompilerParams(
            dimension_semantics=("arbitrary",),
            vmem_limit_bytes=VMEM_LIMIT),
        name="mixers",
    )(zb, zb, zb, zf, lb, og_h,
      zb, zb, zb, zf, zf, b0, b1, gain_m,
      zb, zb, zb, qg, kg, sinks, bias,
      zb, zb, zb, zb, zb, zb, zb, conv_w, alog, dt, gain_d, *consts,
      zb, zb, zb, zb, x3, wbr, wo)


def _mlp_kernel(x_ref, g_ref, wu_ref, wd_ref, y_ref, h_ref, acc_ref):
    j = pl.program_id(1)

    @pl.when(j == 0)
    def _():
        x = x_ref[...]
        ms = jnp.mean(x * x, axis=-1, keepdims=True)
        h_ref[...] = _bf((x * lax.rsqrt(ms + EPS)) * g_ref[...])
        acc_ref[...] = jnp.zeros(acc_ref.shape, F32)

    up = jnp.dot(h_ref[...], wu_ref[...], preferred_element_type=F32)
    act = jnp.square(jnp.maximum(up, 0.0))
    acc_ref[...] += jnp.dot(_bf(act), wd_ref[...], preferred_element_type=F32)

    @pl.when(j == pl.num_programs(1) - 1)
    def _():
        y_ref[...] = x_ref[...] + acc_ref[...]


def _mlp(x2, g, wu, wd, tm, tf):
    n = x2.shape[0]
    return pl.pallas_call(
        _mlp_kernel,
        grid=(n // tm, D_FF // tf),
        in_specs=[pl.BlockSpec((tm, D_MODEL), lambda i, j: (i, 0)),
                  pl.BlockSpec((1, D_MODEL), lambda i, j: (0, 0)),
                  pl.BlockSpec((D_MODEL, tf), lambda i, j: (0, j)),
                  pl.BlockSpec((tf, D_MODEL), lambda i, j: (j, 0))],
        out_specs=pl.BlockSpec((tm, D_MODEL), lambda i, j: (i, 0)),
        out_shape=jax.ShapeDtypeStruct((n, D_MODEL), F32),
        scratch_shapes=[pltpu.VMEM((tm, D_MODEL), BF16), pltpu.VMEM((tm, D_MODEL), F32)],
        compiler_params=pltpu.CompilerParams(
            dimension_semantics=("arbitrary", "arbitrary"), vmem_limit_bytes=VMEM_LIMIT),
        name="mlp",
    )(x2, g, wu, wd)


def _t5_bucket_table():
    n = np.arange(WINDOW)
    max_exact = N_BUCKETS // 2
    nf = np.maximum(n, max_exact).astype(np.float32)
    large = max_exact + (np.log(nf / np.float32(max_exact)) / np.float32(math.log(MAX_DISTANCE / max_exact))
                         * (N_BUCKETS - max_exact)).astype(np.int32)
    large = np.minimum(large, N_BUCKETS - 1)
    return np.where(n < max_exact, n, large)


def _swa_bias(rel_table):
    per_dist = rel_table.astype(F32)[_t5_bucket_table()].T
    n_heads = per_dist.shape[0]
    span = 3 * WINDOW
    pad_lo = jnp.full((n_heads, WINDOW - 1), NEG_BIG, F32)
    pad_hi = jnp.full((n_heads, span - 2 * WINDOW + 1), NEG_BIG, F32)
    v = jnp.concatenate([pad_lo, per_dist, pad_hi], axis=1)
    hank = jnp.tile(v, (1, WINDOW + 1))[:, :WINDOW * (span + 1)].reshape(n_heads, WINDOW, span + 1)
    return hank[:, :, :2 * WINDOW][:, :, ::-1]


def _lane_row(vals, offset):
    return jnp.zeros((1, LANES), F32).at[0, offset:offset + vals.shape[0]].set(vals.astype(F32))


def _layout_w_in(w):
    w = _bf(w)
    sizes = (512, 512, 512, 512, 256, 256, 512, 4, 4, 512, 512, 128, 128, 1536, 4, 4, 512, 4096)
    parts, start = [], 0
    for s in sizes:
        parts.append(w[:, start:start + s])
        start += s
    (hq, hf, hi, hg, mq, mk, mv, mi, mf, mo, aq, ak, av, dqkv, db, da, dz, gate) = parts
    zpad = lambda n: jnp.zeros((w.shape[0], n), w.dtype)
    wb = jnp.concatenate([hq, hi, hg, mq, mk, mv, mo, aq, ak, av, zpad(256), dqkv, dz, gate], axis=1)
    wf = jnp.concatenate([hf, mi, db, zpad(LANES - 8), mf, da, zpad(LANES - 8)], axis=1)
    return wb, wf


def kernel(x, norm_mix_g, w_in, hgrn_lb_table, hgrn_out_g, mlstm_if_bias, mlstm_out_g,
           attn_q_norm_g, attn_k_norm_g, attn_sinks, rel_bias_table, dn_conv_w, dn_a_log,
           dn_dt_bias, dn_out_g, w_branch, w_out, norm_mlp_g, w_up, w_down):
    batch, seq, d = x.shape
    depth = w_in.shape[0]
    n = batch * seq
    tb = 256
    assert seq % tb == 0 and d == D_MODEL
    tm = 1024 if n % 1024 == 0 else 256

    lb_p = jax.nn.softmax(hgrn_lb_table.astype(F32), axis=0)
    lower_bounds = jnp.cumsum(lb_p, axis=0) - lb_p[0]
    bias = _swa_bias(rel_bias_table)

    x2 = x.reshape(n, d)
    for l in range(depth):
        wb, wf = _layout_w_in(w_in[l])
        zb, zf = _inproj(x2, norm_mix_g[l].reshape(1, d), wb, wf, tm, 2560)

        x2 = _mixers(
            zb.reshape(batch, seq, ZB_W), zf.reshape(batch, seq, ZF_W), x2.reshape(batch, seq, d),
            lower_bounds[l].reshape(1, -1), hgrn_out_g[l].reshape(1, -1).astype(F32),
            _lane_row(mlstm_if_bias[l, 0], 0), _lane_row(mlstm_if_bias[l, 1], 0),
            mlstm_out_g[l].reshape(1, -1).astype(F32),
            jnp.tile(attn_q_norm_g[l].astype(F32), AT_HEADS).reshape(1, -1),
            jnp.tile(attn_k_norm_g[l].astype(F32), 2).reshape(1, -1),
            jnp.broadcast_to(attn_sinks[l].astype(F32)[:, None], (AT_HEADS, LANES)), bias,
            dn_conv_w[l].astype(F32), _lane_row(dn_a_log[l], N_HEADS),
            _lane_row(dn_dt_bias[l], N_HEADS), dn_out_g[l].reshape(1, -1).astype(F32),
            _bf(w_branch[l]), _bf(w_out[l]), batch, seq, tb).reshape(n, d)
        x2 = _mlp(x2, norm_mlp_g[l].reshape(1, d), _bf(w_up[l]), _bf(w_down[l]), tm, 2048)
    return x2.reshape(batch, seq, d)
```

```python
import functools
import math

import numpy as np
import jax
import jax.numpy as jnp
from jax import lax
from jax.experimental import pallas as pl
from jax.experimental.pallas import tpu as pltpu

F32 = jnp.float32
BF16 = jnp.bfloat16

D_MODEL = 1024
EPS = 1e-6
CHUNK = 64
NEG_BIG = -1e30
TINY = 1e-30
LANES = 128
HEAD_W = 128
N_HEADS = 4
BRANCH_W = 512
N_BRANCH = 4
D_FF = 4 * D_MODEL

AT_HEADS = 8
AT_HD = 64
WINDOW = 128
N_BUCKETS = 32
MAX_DISTANCE = 128
CONV_K = 4
ML_DK = 64
DN_DK = 128
CHUNK_SHIFT = 6
AT_HD_SHIFT = 6
LOG2_E = 1.4426950408889634
CONV_TAIL = 16

ZB_HQ, ZB_HI, ZB_HG = 0, 512, 1024
ZB_MQK, ZB_MV, ZB_MO = 1536, 2048, 2560
ZB_AQ, ZB_AKV = 3072, 3584
ZB_DQ, ZB_DK, ZB_DV, ZB_DZ = 4096, 4608, 5120, 5632
ZB_GATE = 6144
ZB_W = 10240
ZF_HF, ZF_S1, ZF_S2 = 0, 512, 640
ZF_W = 768

VMEM_LIMIT = 56 * 1024 * 1024


def _bf(x):
    return x.astype(BF16)


def _mm(a, b):
    return jnp.dot(_bf(a), _bf(b), preferred_element_type=F32)


def _mm_nt(a, b):
    return lax.dot_general(_bf(a), _bf(b), (((1,), (1,)), ((), ())),
                           preferred_element_type=F32)


def _mm_tn(a, b):
    return lax.dot_general(_bf(a), _bf(b), (((0,), (0,)), ((), ())),
                           preferred_element_type=F32)


def _split3(x):
    hi = _bf(x)
    r = x - hi.astype(F32)
    mid = _bf(r)
    lo = _bf(r - mid.astype(F32))
    return hi, mid, lo


def _mm3(sel, x):
    hi, mid, lo = _split3(x)
    return (jnp.dot(sel, hi, preferred_element_type=F32)
            + jnp.dot(sel, mid, preferred_element_type=F32)
            + jnp.dot(sel, lo, preferred_element_type=F32))


def _mm3_nt(sel, x):
    dn = (((1,), (1,)), ((), ()))
    hi, mid, lo = _split3(x)
    return (lax.dot_general(sel, hi, dn, preferred_element_type=F32)
            + lax.dot_general(sel, mid, dn, preferred_element_type=F32)
            + lax.dot_general(sel, lo, dn, preferred_element_type=F32))


def _sigmoid(x):
    return 1.0 / (1.0 + jnp.exp2(x * (-LOG2_E)))


def _silu(x):
    return x * _sigmoid(x)


def _log1pexp_neg_abs(x):
    return jnp.log(1.0 + jnp.exp(-jnp.abs(x)))


def _log_sigmoid(x):
    return jnp.minimum(x, 0.0) - _log1pexp_neg_abs(x)


def _softplus(x):
    return jnp.maximum(x, 0.0) + _log1pexp_neg_abs(x)


def _iota(shape, dim):
    return lax.broadcasted_iota(jnp.int32, shape, dim)


def _chunk_tril(n):
    t = _iota((n, n), 0)
    s = _iota((n, n), 1)
    return ((t >> CHUNK_SHIFT) == (s >> CHUNK_SHIFT)) & (s <= t)


def _head_rms(o):
    return o * lax.rsqrt(jnp.mean(o * o, axis=-1, keepdims=True) + EPS)


SUBLANES = 8
SUB_LEVELS = 2
CHUNK_TRIL = 2 * SUB_LEVELS


def _mixer_constants(tb):
    t = np.arange(tb)[:, None]
    r = np.arange(tb)[None, :]
    segm = []
    for lv in range(1, SUB_LEVELS + 1):
        same = (t >> lv) == (r >> lv)
        segm += [same & (r <= t), same & (r > t)]
    segm.append(((t >> CHUNK_SHIFT) == (r >> CHUNK_SHIFT)) & (r <= t))
    lane = np.arange(BRANCH_W)
    block = lambda n, w: (lane[:n, None] // w) == (lane[None, :n] // w)
    as_bf16 = lambda m: jnp.asarray(np.asarray(m, np.float32), BF16)
    return (as_bf16(np.stack(segm)),
            as_bf16(np.eye(16, LANES)),
            as_bf16(block(BRANCH_W, AT_HD)),
            as_bf16(block(LANES, AT_HD)),
            as_bf16(block(BRANCH_W, HEAD_W)))


def _inproj_kernel(x_ref, g_ref, wb_ref, wf_ref, zb_ref, zf_ref, h_ref):
    @pl.when(pl.program_id(1) == 0)
    def _():
        x = x_ref[...]
        ms = jnp.mean(x * x, axis=-1, keepdims=True)
        h = _bf((x * lax.rsqrt(ms + EPS)) * g_ref[...])
        h_ref[...] = h
        zf_ref[...] = jnp.dot(h, wf_ref[...], preferred_element_type=F32)

    zb_ref[...] = _bf(jnp.dot(h_ref[...], wb_ref[...], preferred_element_type=F32))


def _inproj(x2, g, wb, wf, tm, tn):
    n, d = x2.shape
    return pl.pallas_call(
        _inproj_kernel,
        grid=(n // tm, ZB_W // tn),
        in_specs=[
            pl.BlockSpec((tm, d), lambda i, j: (i, 0)),
            pl.BlockSpec((1, d), lambda i, j: (0, 0)),
            pl.BlockSpec((d, tn), lambda i, j: (0, j)),
            pl.BlockSpec((d, ZF_W), lambda i, j: (0, 0)),
        ],
        out_specs=[
            pl.BlockSpec((tm, tn), lambda i, j: (i, j)),
            pl.BlockSpec((tm, ZF_W), lambda i, j: (i, 0)),
        ],
        out_shape=[jax.ShapeDtypeStruct((n, ZB_W), BF16),
                   jax.ShapeDtypeStruct((n, ZF_W), F32)],
        scratch_shapes=[pltpu.VMEM((tm, d), BF16)],
        compiler_params=pltpu.CompilerParams(
            dimension_semantics=("arbitrary", "arbitrary"),
            vmem_limit_bytes=VMEM_LIMIT),
        name="inproj",
    )(x2, g, wb, wf)


def _hgrn_kernel(t_idx, q_ref, v_ref, g_ref, f_ref, lb_ref, og_ref, segm_ref, o_ref, st_ref, *, tb):
    @pl.when(t_idx == 0)
    def _():
        st_ref[...] = jnp.zeros(st_ref.shape, F32)

    n_chunks = tb // CHUNK
    z = f_ref[...]
    lb = lb_ref[...]
    sig = _sigmoid(z)
    f = lb + (1.0 - lb) * sig
    logf2 = jnp.log(jnp.maximum(f, TINY)) * LOG2_E
    k = (1.0 - lb) * (1.0 - sig)
    q = _silu(q_ref[...].astype(F32))

    hi = _bf(logf2)
    lo = _bf(logf2 - hi.astype(F32))

    def seg_sum(idx):
        sel = segm_ref[idx]
        return (jnp.dot(sel, hi, preferred_element_type=F32)
                + jnp.dot(sel, lo, preferred_element_type=F32))

    n_lv = CHUNK_SHIFT
    cum2 = seg_sum(CHUNK_TRIL)
    yield

    def rows_of(row_of_group):
        pieces = []
        for g in range(tb // SUBLANES):
            r = row_of_group(g)
            pieces.append(jnp.zeros((SUBLANES, BRANCH_W), F32) if r is None
                          else jnp.broadcast_to(cum2[r:r + 1, :], (SUBLANES, BRANCH_W)))
        return jnp.concatenate(pieces, axis=0)

    def seg_pair(lv):
        m = 1 << lv
        if m < SUBLANES:
            return seg_sum(2 * (lv - 1)), seg_sum(2 * (lv - 1) + 1)
        per = m // SUBLANES
        start = lambda g: (g // per) * m
        base = rows_of(lambda g: None if start(g) % CHUNK == 0 else start(g) - 1)
        top = rows_of(lambda g: start(g) + m - 1)
        return cum2 - base, top - cum2

    row = _iota((tb, 1), 0)
    qd, kd = [], []
    for lv in range(n_lv):
        upper = ((row >> lv) & 1) == 1
        if lv == 0:
            qd.append(_bf(q * jnp.exp2(jnp.where(upper, logf2, NEG_BIG))))
            kd.append(_bf(jnp.where(upper, 0.0, k)))
        else:
            prefix, suffix = seg_pair(lv)
            qd.append(_bf(q * jnp.exp2(jnp.where(upper, prefix, NEG_BIG))))
            yield
            kd.append(_bf(k * jnp.exp2(jnp.where(upper, NEG_BIG, suffix))))
        yield
    q_in = _bf(q * jnp.exp2(cum2))
    yield
    k_out = _bf(k * jnp.exp2(seg_pair(n_lv)[1]))
    qk = _bf(q * k)
    ones_b = jnp.ones((HEAD_W, CHUNK), BF16)
    t_i = _iota((CHUNK, CHUNK), 0)
    s_i = _iota((CHUNK, CHUNK), 1)
    block_masks = [(t_i >> (lv + 1)) == (s_i >> (lv + 1)) for lv in range(n_lv)]
    eye = t_i == s_i

    heads = range(N_HEADS)
    cs = [slice(h * HEAD_W, (h + 1) * HEAD_W) for h in heads]
    rs = [slice(ci * CHUNK, (ci + 1) * CHUNK) for ci in range(n_chunks)]
    cells = [(ci, h) for ci in range(n_chunks) for h in heads]
    a = {(ci, h): jnp.where(eye, jnp.dot(qk[rs[ci], cs[h]], ones_b, preferred_element_type=F32), 0.0)
         for ci, h in cells}
    yield
    for lv in range(n_lv):
        a = {(ci, h): a[ci, h] + jnp.where(
            block_masks[lv], _mm_nt(qd[lv][rs[ci], cs[h]], kd[lv][rs[ci], cs[h]]), 0.0)
            for ci, h in cells}
        yield
    intra = {(ci, h): _mm(a[ci, h], v_ref[rs[ci], cs[h]]) for ci, h in cells}
    upd = {(ci, h): _mm_tn(v_ref[rs[ci], cs[h]], k_out[rs[ci], cs[h]]) for ci, h in cells}
    yield

    st = [st_ref[h] for h in heads]
    for ci in range(n_chunks):
        o = [intra[ci, h] + _mm_nt(q_in[rs[ci], cs[h]], st[h]) for h in heads]
        total = [cum2[(ci + 1) * CHUNK - 1:(ci + 1) * CHUNK, cs[h]] for h in heads]
        st = [jnp.exp2(total[h]) * st[h] + upd[ci, h] for h in heads]
        rms = [_head_rms(o[h]) for h in heads]
        for h in heads:
            out = rms[h] * og_ref[:, cs[h]] * _silu(g_ref[rs[ci], cs[h]].astype(F32))
            o_ref[rs[ci], cs[h]] = _bf(out)
        yield
    for h in heads:
        st_ref[h] = st[h]


def _mlstm_kernel(t_idx, qk_ref, v_ref, og_pre_ref, s1_ref, s2_ref, b0_ref, b1_ref, gain_ref,
                  segm_ref, eye_ref, o_ref, c_ref, m_ref, *, tb):
    @pl.when(t_idx == 0)
    def _():
        c_ref[...] = jnp.zeros(c_ref.shape, F32)
        m_ref[...] = jnp.zeros(m_ref.shape, F32)

    li_all = s1_ref[...] + b0_ref[...]
    lf_all = _log_sigmoid(s2_ref[...] + b1_ref[...])
    cum_all = _mm3(segm_ref[CHUNK_TRIL], lf_all)
    rr_all = li_all - cum_all
    eye = eye_ref[...]
    causal = _iota((CHUNK, CHUNK), 1) <= _iota((CHUNK, CHUNK), 0)
    lane = _iota((1, LANES), 1)
    ones_b = jnp.ones((CHUNK, LANES), BF16)
    ones_sq = jnp.ones((LANES, LANES), BF16)
    cum_bs = [jnp.broadcast_to(cum_all[:, h:h + 1], (tb, LANES)) for h in range(N_HEADS)]
    li_bs = [jnp.broadcast_to(li_all[:, h:h + 1], (tb, LANES)) for h in range(N_HEADS)]

    n_chunks = tb // CHUNK
    heads = range(N_HEADS)
    cells = [(ci, h) for ci in range(n_chunks) for h in heads]
    rs = [slice(ci * CHUNK, (ci + 1) * CHUNK) for ci in range(n_chunks)]
    vs = [slice(h * HEAD_W, (h + 1) * HEAD_W) for h in heads]
    halves = [(lane >> AT_HD_SHIFT) == i for i in range(2)]
    rr_t = [_mm3_nt(eye, rr_all[rs[ci]]) for ci in range(n_chunks)]
    yield
    qm, ks, v_aug, cum_b = {}, {}, {}, {}
    for ci, h in cells:
        gs = slice((h // 2) * LANES, (h // 2 + 1) * LANES)
        qm[ci, h] = jnp.where(halves[h % 2], qk_ref[rs[ci], gs], jnp.zeros((), BF16))
        ks[ci, h] = qk_ref[rs[ci], 2 * LANES + gs.start:2 * LANES + gs.stop].astype(F32) * (ML_DK ** -0.5)
        v_aug[ci, h] = jnp.concatenate([v_ref[rs[ci], vs[h]], ones_b], axis=1)
        cum_b[ci, h] = cum_bs[h][rs[ci]]
    logd = {c: jnp.where(causal, cum_b[c][:, :CHUNK] + rr_t[c[0]][c[1]:c[1] + 1, :], NEG_BIG)
            for c in cells}
    rowmax_b = {c: jnp.broadcast_to(jnp.max(logd[c], axis=-1, keepdims=True), (CHUNK, LANES))
                for c in cells}
    yield
    qk = {c: _mm_nt(qm[c], ks[c]) for c in cells}
    yield
    s0 = {c: _bf(qk[c] * jnp.exp(logd[c] - rowmax_b[c][:, :CHUNK])) for c in cells}
    yield
    p0 = {c: jnp.dot(s0[c], v_aug[c], preferred_element_type=F32) for c in cells}
    yield
    mref_b = {c: rowmax_b[c][CHUNK - 1:CHUNK, :] for c in cells}
    cum_last_b = {c: cum_b[c][CHUNK - 1:CHUNK, :] for c in cells}
    kw = {c: _bf(ks[c] * jnp.exp(cum_last_b[c] - cum_b[c] + li_bs[c[1]][rs[c[0]]] - mref_b[c]))
          for c in cells}
    yield
    upd = {c: _mm_tn(kw[c], v_aug[c]) for c in cells}
    yield

    cms = [c_ref[h] for h in heads]
    ms = [m_ref[h:h + 1, :] for h in heads]
    for ci in range(n_chunks):
        m_inter = [cum_b[ci, h] + ms[h] for h in heads]
        m_t = [jnp.maximum(m_inter[h], rowmax_b[ci, h]) for h in heads]
        qc = [_mm(qm[ci, h], cms[h]) for h in heads]
        e2 = [jnp.exp(rowmax_b[ci, h] - m_t[h]) for h in heads]
        wi = [jnp.exp(m_inter[h] - m_t[h]) for h in heads]
        m_new = [m_t[h][CHUNK - 1:CHUNK, :] for h in heads]
        dec = [jnp.exp(cum_last_b[ci, h] + ms[h] - m_new[h]) for h in heads]
        sc = [jnp.exp(mref_b[ci, h] - m_new[h]) for h in heads]
        cms = [jnp.concatenate([dec[h], dec[h]], axis=1) * cms[h]
               + jnp.concatenate([sc[h], sc[h]], axis=1) * upd[ci, h] for h in heads]
        ms = m_new
        yield
        num = [e2[h] * p0[ci, h][:, :HEAD_W] + wi[h] * qc[h][:, :HEAD_W] for h in heads]
        den = [e2[h] * p0[ci, h][:, HEAD_W:] + wi[h] * qc[h][:, HEAD_W:] for h in heads]
        hval = [num[h] / jnp.maximum(jnp.abs(den[h]), jnp.exp(-m_t[h])) for h in heads]
        mean_sq = [_mm(hval[h] * hval[h], ones_sq) * (1.0 / HEAD_W) for h in heads]
        for h in heads:
            out = (hval[h] * lax.rsqrt(mean_sq[h] + EPS) * gain_ref[:, vs[h]]
                   * _sigmoid(og_pre_ref[rs[ci], vs[h]].astype(F32)))
            o_ref[rs[ci], vs[h]] = _bf(out)
        yield
    for h in heads:
        c_ref[h] = cms[h]
        m_ref[h:h + 1, :] = ms[h]


def _swa_kernel(t_idx, q_ref, kvp_ref, kvc_ref, qg_ref, kg_ref, sink_ref, bias_ref, segq_ref,
                segk_ref, o_ref, *, n_win):
    first_cols = jnp.where(t_idx == 0, WINDOW, 0)

    qf = q_ref[...].astype(F32)
    ms_q = _mm(qf * qf, segq_ref[...]) * (1.0 / AT_HD)
    qn = _bf(qf * lax.rsqrt(ms_q + EPS) * qg_ref[...] * (AT_HD ** -0.5))

    kw = jnp.concatenate([kvp_ref[:, :LANES], kvc_ref[:, :LANES]], axis=0).astype(F32)
    vw = jnp.concatenate([kvp_ref[:, LANES:], kvc_ref[:, LANES:]], axis=0).astype(F32)
    ms_k = _mm(kw * kw, segk_ref[...]) * (1.0 / AT_HD)
    kn = kw * lax.rsqrt(ms_k + EPS) * kg_ref[...]
    lane = _iota((1, LANES), 1)
    halves = [(lane >> AT_HD_SHIFT) == i for i in range(2)]
    k_src = {True: _bf(kn), False: _bf(pltpu.roll(kn, AT_HD, 1))}
    v_roll = pltpu.roll(vw, AT_HD, 1)
    v_src = {(same, i): _bf(jnp.where(halves[i], vw if same else v_roll, 0.0))
             for same in (True, False) for i in range(2)}

    kpos = _iota((1, 2 * WINDOW), 1)
    yield
    heads = range(AT_HEADS)
    same = [(h % 2) == h // (AT_HEADS // 2) for h in heads]
    sink = [sink_ref[h:h + 1, 0:1] for h in heads]
    cells = [(w, h) for w in range(n_win) for h in heads]
    qs = [slice(w * WINDOW, (w + 1) * WINDOW) for w in range(n_win)]
    ws = [slice(w * WINDOW, (w + 2) * WINDOW) for w in range(n_win)]
    qh = {(w, h): jnp.where(halves[h % 2], qn[qs[w], (h // 2) * LANES:(h // 2 + 1) * LANES],
                            jnp.zeros((), BF16)) for w, h in cells}
    lg = {(w, h): _mm_nt(qh[w, h], k_src[same[h]][ws[w]]) + bias_ref[h] for w, h in cells}
    for h in heads:
        lg[0, h] = jnp.where(kpos < first_cols, NEG_BIG, lg[0, h])
    yield
    mx = {(w, h): jnp.maximum(jnp.max(lg[w, h], axis=-1, keepdims=True), sink[h]) for w, h in cells}
    yield
    p = {c: jnp.exp(lg[c] - mx[c]) for c in cells}
    yield
    denom = {(w, h): jnp.sum(p[w, h], axis=-1, keepdims=True) + jnp.exp(sink[h] - mx[w, h])
             for w, h in cells}
    pv = {(w, h): _mm(p[w, h], v_src[(same[h], h % 2)][ws[w]]) for w, h in cells}
    yield
    for w in range(n_win):
        for pair in range(AT_HEADS // 2):
            acc = (pv[w, 2 * pair] / denom[w, 2 * pair]
                   + pv[w, 2 * pair + 1] / denom[w, 2 * pair + 1])
            o_ref[qs[w], pair * LANES:(pair + 1) * LANES] = _bf(acc)
    yield


def _conv_silu(x_ref, prev_ref, w_ref, idx, has_prev):
    tb = x_ref.shape[0]
    x = x_ref[...].astype(F32)
    prev = prev_ref[CONV_TAIL - SUBLANES:, :].astype(F32) * has_prev
    x2 = jnp.concatenate([prev, x], axis=0)
    w = w_ref[:, idx * BRANCH_W:(idx + 1) * BRANCH_W]
    y = x * w[CONV_K - 1:CONV_K, :]
    for k in range(1, CONV_K):
        y = y + x2[SUBLANES - k:SUBLANES - k + tb, :] * w[CONV_K - 1 - k:CONV_K - k, :]
    return _silu(y)


def _dn_kernel(t_idx, q_ref, k_ref, v_ref, qp_ref, kp_ref, vp_ref, z_ref, s1_ref, s2_ref, w_ref,
               alog_ref, dt_ref, gain_ref, segm_ref, eye_ref, segh_ref, o_ref, st_ref, *, tb):
    @pl.when(t_idx == 0)
    def _():
        st_ref[...] = jnp.zeros(st_ref.shape, F32)

    n_chunks = tb // CHUNK
    has_prev = jnp.where(t_idx > 0, 1.0, 0.0)
    q = _conv_silu(q_ref, qp_ref, w_ref, 0, has_prev)
    yield
    k = _conv_silu(k_ref, kp_ref, w_ref, 1, has_prev)
    yield
    v = _conv_silu(v_ref, vp_ref, w_ref, 2, has_prev)
    yield
    seg = segh_ref[...]
    q = q * lax.rsqrt(_mm(q * q, seg) + EPS) * (DN_DK ** -0.5)
    k = k * lax.rsqrt(_mm(k * k, seg) + EPS)
    yield

    beta_all = _sigmoid(s1_ref[...])
    g_all = -jnp.exp(alog_ref[...]) * _softplus(s2_ref[...] + dt_ref[...])
    gam_all = _mm3(segm_ref[CHUNK_TRIL], g_all)
    gam_t = _mm3_nt(eye_ref[...], gam_all)
    tile = 2 * CHUNK
    n_tiles = tb // tile
    t_sq = _iota((tile, tile), 0)
    s_sq = _iota((tile, tile), 1)
    incl_t = _chunk_tril(tile)
    level_masks = []
    for lv in range(CHUNK_SHIFT):
        m = 1 << lv
        level_masks.append(((t_sq >> (lv + 1)) == (s_sq >> (lv + 1)))
                           & ((t_sq & m) != 0) & ((s_sq & m) == 0))

    heads = range(N_HEADS)
    cs = [slice(h * HEAD_W, (h + 1) * HEAD_W) for h in heads]
    gcol = {h: gam_all[:, N_HEADS + h:N_HEADS + h + 1] for h in heads}
    bcol = {h: beta_all[:, N_HEADS + h:N_HEADS + h + 1] for h in heads}
    q_dec = {h: q[:, cs[h]] * jnp.exp(gcol[h]) for h in heads}
    sol, attn = {}, {}
    yield

    def prep(ti):
        rt = slice(ti * tile, (ti + 1) * tile)
        decay = {h: jnp.exp(jnp.where(
            incl_t, gcol[h][rt] - gam_t[N_HEADS + h:N_HEADS + h + 1, rt], NEG_BIG)) for h in heads}
        kb = {h: _bf(k[rt, cs[h]]) for h in heads}
        low = {h: bcol[h][rt] * _mm_nt(kb[h], kb[h]) * decay[h] for h in heads}
        yield
        n_mat = {h: -jnp.where(level_masks[0], low[h], 0.0) for h in heads}
        for lm in level_masks[1:]:
            x_m = {}
            for h in heads:
                l_m = jnp.where(lm, low[h], 0.0)
                x_m[h] = l_m + _mm(l_m, n_mat[h])
            yield
            for h in heads:
                n_mat[h] = n_mat[h] - x_m[h] - _mm(n_mat[h], x_m[h])
            yield
        for h in heads:
            bc = bcol[h][rt]
            rhs = jnp.concatenate([v[rt, cs[h]] * bc,
                                   k[rt, cs[h]] * (bc * jnp.exp(gcol[h][rt]))], axis=1)
            sol[ti, h] = rhs + _mm(n_mat[h], rhs)
            attn[ti, h] = _bf(_mm_nt(q[rt, cs[h]], kb[h]) * decay[h])
        yield

    st = [st_ref[h] for h in heads]
    per_tile = tile // CHUNK
    vnew = {}

    def recur(ci):
        rs = slice(ci * CHUNK, (ci + 1) * CHUNK)
        ti, cj = divmod(ci, per_tile)
        rl = slice(cj * CHUNK, (cj + 1) * CHUNK)
        pad = [jnp.zeros((CHUNK, HEAD_W), BF16)] * (per_tile - cj - 1)
        if cj == 0:
            for h in heads:
                vnew[h] = []
        g_last = [gcol[h][ci * CHUNK + CHUNK - 1:(ci + 1) * CHUNK, :] for h in heads]
        k_dec = [_bf(k[rs, cs[h]] * jnp.exp(g_last[h] - gcol[h][rs])) for h in heads]
        ws = [_mm(jnp.concatenate([sol[ti, h][rl, HEAD_W:], q_dec[h][rs]], axis=0), st[h])
              for h in heads]
        v_new = [_bf(sol[ti, h][rl, :HEAD_W] - ws[h][:CHUNK]) for h in heads]
        for h in heads:
            vnew[h].append(v_new[h])
        o = [ws[h][CHUNK:] + jnp.dot(attn[ti, h][rl], jnp.concatenate(vnew[h] + pad, axis=0),
                                      preferred_element_type=F32) for h in heads]
        d_st = [_mm_tn(k_dec[h], v_new[h]) for h in heads]
        for h in heads:
            st[h] = jnp.exp(g_last[h]) * st[h] + d_st[h]
        rms = [_head_rms(o[h]) for h in heads]
        for h in heads:
            out = rms[h] * gain_ref[...] * _silu(z_ref[rs, cs[h]].astype(F32))
            o_ref[rs, cs[h]] = _bf(out)

    for _ in zip(*[prep(ti) for ti in range(n_tiles)]):
        yield
    for ci in range(n_chunks):
        recur(ci)
        yield
    for h in heads:
        st_ref[h] = st[h]


def _round_robin(gens):
    gens = list(gens)
    while gens:
        for g in list(gens):
            try:
                next(g)
            except StopIteration:
                gens.remove(g)


def _merge_body(o_scr, gate_refs, x_ref, wb_ref, wo_ref, y_ref):
    outs = [o_scr[:, n * BRANCH_W:(n + 1) * BRANCH_W] for n in range(N_BRANCH)]
    yield
    proj = [jnp.dot(outs[n], wb_ref[n], preferred_element_type=F32) for n in range(N_BRANCH)]
    yield
    gates = [_sigmoid(gate_refs[n][...].astype(F32)) for n in range(N_BRANCH)]
    yield
    merged = (gates[0] * proj[0] + gates[1] * proj[1]) + (gates[2] * proj[2] + gates[3] * proj[3])
    yield
    y_ref[...] = x_ref[...] + jnp.dot(_bf(merged), wo_ref[...], preferred_element_type=F32)
    yield


def _mixers_kernel(hq, hv, hg, hf, lb, og_h,
                   mqk, mv, mo, s1, s2, b0, b1, gain_m,
                   aq, kvp, kvc, qg, kg, sinks, bias,
                   dq, dk, dv, dqp, dkp, dvp, dz, conv_w, alog, dt, gain_d,
                   segm, eye16, seg64q, seg64k, seg128,
                   g0, g1, g2, g3, x_prev, wbr, wo,
                   y_ref, o_scr, st_h, c_m, m_m, st_d, *, tb, nt, total):
    s = pl.program_id(0)
    t_idx = lax.rem(jnp.minimum(s, total - 1), nt)

    @pl.when(s == 0)
    def _():
        o_scr[...] = jnp.zeros(o_scr.shape, BF16)

    at = lambda *refs: [r.at[0] for r in refs]
    o_a, o_b, o_c, o_d = (o_scr.at[:, n * BRANCH_W:(n + 1) * BRANCH_W] for n in range(N_BRANCH))
    _round_robin([
        _merge_body(o_scr, at(g0, g1, g2, g3), x_prev.at[0], wbr, wo, y_ref.at[0]),
        _dn_kernel(t_idx, *at(dq, dk, dv, dqp, dkp, dvp, dz, s1, s2), conv_w, alog, dt, gain_d,
                   segm, eye16, seg128, o_d, st_d, tb=tb),
        _hgrn_kernel(t_idx, *at(hq, hv, hg, hf), lb, og_h, segm, o_a, st_h, tb=tb),
        _mlstm_kernel(t_idx, *at(mqk, mv, mo, s1, s2), b0, b1, gain_m, segm, eye16,
                      o_b, c_m, m_m, tb=tb),
        _swa_kernel(t_idx, *at(aq, kvp, kvc), qg, kg, sinks, bias, seg64q, seg64k, o_c,
                    n_win=tb // WINDOW),
    ])


def _mixers(zb, zf, x3, lb, og_h, b0, b1, gain_m, qg, kg, sinks, bias, conv_w, alog, dt, gain_d,
            wbr, wo, batch, seq, tb):
    nt = seq // tb
    total = batch * nt
    cur = lambda s: jnp.minimum(s, total - 1)
    prv = lambda s: jnp.maximum(s - 1, 0)
    bt = lambda i: (i // nt, i % nt)
    blk = lambda c: pl.BlockSpec((1, tb, BRANCH_W), lambda s, c=c: (*bt(cur(s)), c // BRANCH_W))
    sm = lambda c: pl.BlockSpec((1, tb, LANES), lambda s, c=c: (*bt(cur(s)), c // LANES))
    full = lambda shape: pl.BlockSpec(shape, lambda s: (0,) * len(shape))
    kvw = 2 * LANES
    n_win = tb // WINDOW
    per = tb // CONV_TAIL
    kv_prev = pl.BlockSpec(
        (1, WINDOW, kvw),
        lambda s: (cur(s) // nt, jnp.maximum((cur(s) % nt) * n_win - 1, 0), ZB_AKV // kvw))
    dn_prev = lambda c: pl.BlockSpec(
        (1, CONV_TAIL, BRANCH_W),
        lambda s, c=c: (cur(s) // nt, jnp.maximum((cur(s) % nt) * per - 1, 0), c // BRANCH_W))
    gate = lambda k: pl.BlockSpec((1, tb, D_MODEL), lambda s, k=k: (*bt(prv(s)), ZB_GATE // D_MODEL + k))
    x_spec = pl.BlockSpec((1, tb, D_MODEL), lambda s: (*bt(prv(s)), 0))
    consts = _mixer_constants(tb)
    return pl.pallas_call(
        functools.partial(_mixers_kernel, tb=tb, nt=nt, total=total),
        grid=(total + 1,),
        in_specs=[
            blk(ZB_HQ), blk(ZB_HI), blk(ZB_HG), blk(ZF_HF), full((1, BRANCH_W)), full((1, BRANCH_W)),
            blk(ZB_MQK), blk(ZB_MV), blk(ZB_MO), sm(ZF_S1), sm(ZF_S2),
            full((1, LANES)), full((1, LANES)), full((1, BRANCH_W)),
            blk(ZB_AQ), kv_prev, pl.BlockSpec((1, tb, kvw), lambda s: (*bt(cur(s)), ZB_AKV // kvw)),
            full((1, BRANCH_W)), full((1, LANES)), full((AT_HEADS, LANES)),
            full((AT_HEADS, WINDOW, 2 * WINDOW)),
            blk(ZB_DQ), blk(ZB_DK), blk(ZB_DV), dn_prev(ZB_DQ), dn_prev(ZB_DK), dn_prev(ZB_DV),
            blk(ZB_DZ), full((CONV_K, 3 * BRANCH_W)), full((1, LANES)), full((1, LANES)),
            full((1, LANES)),
        ] + [full(c.shape) for c in consts] + [
            gate(0), gate(1), gate(2), gate(3), x_spec,
            full((N_BRANCH, BRANCH_W, D_MODEL)), full((D_MODEL, D_MODEL)),
        ],
        out_specs=x_spec,
        out_shape=jax.ShapeDtypeStruct((batch, seq, D_MODEL), F32),
        scratch_shapes=[pltpu.VMEM((tb, N_BRANCH * BRANCH_W), BF16),
                        pltpu.VMEM((N_HEADS, HEAD_W, HEAD_W), F32),
                        pltpu.VMEM((N_HEADS, LANES, 2 * HEAD_W), F32),
                        pltpu.VMEM((8, LANES), F32),
                        pltpu.VMEM((N_HEADS, DN_DK, HEAD_W), F32)],
        compiler_params=pltpu.CompilerParams(
            dimension_semantics=("arbitrary",),
            vmem_limit_bytes=VMEM_LIMIT),
        name="mixers",
    )(zb, zb, zb, zf, lb, og_h,
      zb, zb, zb, zf, zf, b0, b1, gain_m,
      zb, zb, zb, qg, kg, sinks, bias,
      zb, zb, zb, zb, zb, zb, zb, conv_w, alog, dt, gain_d, *consts,
      zb, zb, zb, zb, x3, wbr, wo)


def _mlp_kernel(x_ref, g_ref, wu_ref, wd_ref, y_ref, h_ref, acc_ref):
    j = pl.program_id(1)

    @pl.when(j == 0)
    def _():
        x = x_ref[...]
        ms = jnp.mean(x * x, axis=-1, keepdims=True)
        h_ref[...] = _bf((x * lax.rsqrt(ms + EPS)) * g_ref[...])
        acc_ref[...] = jnp.zeros(acc_ref.shape, F32)

    up = jnp.dot(h_ref[...], wu_ref[...], preferred_element_type=F32)
    act = jnp.square(jnp.maximum(up, 0.0))
    acc_ref[...] += jnp.dot(_bf(act), wd_ref[...], preferred_element_type=F32)

    @pl.when(j == pl.num_programs(1) - 1)
    def _():
        y_ref[...] = x_ref[...] + acc_ref[...]


def _mlp(x2, g, wu, wd, tm, tf):
    n = x2.shape[0]
    return pl.pallas_call(
        _mlp_kernel,
        grid=(n // tm, D_FF // tf),
        in_specs=[pl.BlockSpec((tm, D_MODEL), lambda i, j: (i, 0)),
                  pl.BlockSpec((1, D_MODEL), lambda i, j: (0, 0)),
                  pl.BlockSpec((D_MODEL, tf), lambda i, j: (0, j)),
                  pl.BlockSpec((tf, D_MODEL), lambda i, j: (j, 0))],
        out_specs=pl.BlockSpec((tm, D_MODEL), lambda i, j: (i, 0)),
        out_shape=jax.ShapeDtypeStruct((n, D_MODEL), F32),
        scratch_shapes=[pltpu.VMEM((tm, D_MODEL), BF16), pltpu.VMEM((tm, D_MODEL), F32)],
        compiler_params=pltpu.CompilerParams(
            dimension_semantics=("arbitrary", "arbitrary"), vmem_limit_bytes=VMEM_LIMIT),
        name="mlp",
    )(x2, g, wu, wd)


def _t5_bucket_table():
    n = np.arange(WINDOW)
    max_exact = N_BUCKETS // 2
    nf = np.maximum(n, max_exact).astype(np.float32)
    large = max_exact + (np.log(nf / np.float32(max_exact)) / np.float32(math.log(MAX_DISTANCE / max_exact))
                         * (N_BUCKETS - max_exact)).astype(np.int32)
    large = np.minimum(large, N_BUCKETS - 1)
    return np.where(n < max_exact, n, large)


def _swa_bias(rel_table):
    per_dist = rel_table.astype(F32)[_t5_bucket_table()].T
    n_heads = per_dist.shape[0]
    span = 3 * WINDOW
    pad_lo = jnp.full((n_heads, WINDOW - 1), NEG_BIG, F32)
    pad_hi = jnp.full((n_heads, span - 2 * WINDOW + 1), NEG_BIG, F32)
    v = jnp.concatenate([pad_lo, per_dist, pad_hi], axis=1)
    hank = jnp.tile(v, (1, WINDOW + 1))[:, :WINDOW * (span + 1)].reshape(n_heads, WINDOW, span + 1)
    return hank[:, :, :2 * WINDOW][:, :, ::-1]


def _lane_row(vals, offset):
    return jnp.zeros((1, LANES), F32).at[0, offset:offset + vals.shape[0]].set(vals.astype(F32))


def _layout_w_in(w):
    sizes = (512, 512, 512, 512, 256, 256, 512, 4, 4, 512, 512, 128, 128, 1536, 4, 4, 512, 4096)
    parts, start = [], 0
    for s in sizes:
        parts.append(_bf(w[:, start:start + s]))
        start += s
    (hq, hf, hi, hg, mq, mk, mv, mi, mf, mo, aq, ak, av, dqkv, db, da, dz, gate) = parts
    zpad = lambda n: jnp.zeros((w.shape[0], n), BF16)
    wb = jnp.concatenate([hq, hi, hg, mq, mk, mv, mo, aq, ak, av, zpad(256), dqkv, dz, gate], axis=1)
    wf = jnp.concatenate([hf, mi, db, zpad(LANES - 8), mf, da, zpad(LANES - 8)], axis=1)
    return wb, wf


def kernel(x, norm_mix_g, w_in, hgrn_lb_table, hgrn_out_g, mlstm_if_bias, mlstm_out_g,
           attn_q_norm_g, attn_k_norm_g, attn_sinks, rel_bias_table, dn_conv_w, dn_a_log,
           dn_dt_bias, dn_out_g, w_branch, w_out, norm_mlp_g, w_up, w_down):
    batch, seq, d = x.shape
    depth = w_in.shape[0]
    n = batch * seq
    tb = 256
    assert seq % tb == 0 and d == D_MODEL
    tm = 1024 if n % 1024 == 0 else 256

    lb_p = jax.nn.softmax(hgrn_lb_table.astype(F32), axis=0)
    lower_bounds = jnp.cumsum(lb_p, axis=0) - lb_p[0]
    bias = _swa_bias(rel_bias_table)

    x2 = x.reshape(n, d)
    for l in range(depth):
        wb, wf = _layout_w_in(w_in[l])
        zb, zf = _inproj(x2, norm_mix_g[l].reshape(1, d), wb, wf, tm, 2560)

        x2 = _mixers(
            zb.reshape(batch, seq, ZB_W), zf.reshape(batch, seq, ZF_W), x2.reshape(batch, seq, d),
            lower_bounds[l].reshape(1, -1), hgrn_out_g[l].reshape(1, -1).astype(F32),
            _lane_row(mlstm_if_bias[l, 0], 0), _lane_row(mlstm_if_bias[l, 1], 0),
            mlstm_out_g[l].reshape(1, -1).astype(F32),
            jnp.tile(attn_q_norm_g[l].astype(F32), AT_HEADS).reshape(1, -1),
            jnp.tile(attn_k_norm_g[l].astype(F32), 2).reshape(1, -1),
            jnp.broadcast_to(attn_sinks[l].astype(F32)[:, None], (AT_HEADS, LANES)), bias,
            dn_conv_w[l].astype(F32), _lane_row(dn_a_log[l], N_HEADS),
            _lane_row(dn_dt_bias[l], N_HEADS), dn_out_g[l].reshape(1, -1).astype(F32),
            _bf(w_branch[l]), _bf(w_out[l]), batch, seq, tb).reshape(n, d)
        x2 = _mlp(x2, norm_mlp_g[l].reshape(1, d), _bf(w_up[l]), _bf(w_down[l]), tm, 2048)
    return x2.reshape(batch, seq, d)
```

```python
import functools
import math

import numpy as np
import jax
import jax.numpy as jnp
from jax import lax
from jax.experimental import pallas as pl
from jax.experimental.pallas import tpu as pltpu

F32 = jnp.float32
BF16 = jnp.bfloat16

D_MODEL = 1024
EPS = 1e-6
CHUNK = 64
NEG_BIG = -1e30
TINY = 1e-30
LANES = 128
HEAD_W = 128
N_HEADS = 4
BRANCH_W = 512
N_BRANCH = 4
D_FF = 4 * D_MODEL

AT_HEADS = 8
AT_HD = 64
WINDOW = 128
N_BUCKETS = 32
MAX_DISTANCE = 128
CONV_K = 4
ML_DK = 64
DN_DK = 128
CHUNK_SHIFT = 6
AT_HD_SHIFT = 6
LOG2_E = 1.4426950408889634
CONV_TAIL = 16

ZB_HQ, ZB_HI, ZB_HG = 0, 512, 1024
ZB_MQK, ZB_MV, ZB_MO = 1536, 2048, 2560
ZB_AQ, ZB_AKV = 3072, 3584
ZB_DQ, ZB_DK, ZB_DV, ZB_DZ = 4096, 4608, 5120, 5632
ZB_GATE = 6144
ZB_W = 10240
ZF_HF, ZF_S1, ZF_S2 = 0, 512, 640
ZF_W = 768

VMEM_LIMIT = 56 * 1024 * 1024


def _bf(x):
    return x.astype(BF16)


def _mm(a, b):
    return jnp.dot(_bf(a), _bf(b), preferred_element_type=F32)


def _mm_nt(a, b):
    return lax.dot_general(_bf(a), _bf(b), (((1,), (1,)), ((), ())),
                           preferred_element_type=F32)


def _mm_tn(a, b):
    return lax.dot_general(_bf(a), _bf(b), (((0,), (0,)), ((), ())),
                           preferred_element_type=F32)


def _split3(x):
    hi = _bf(x)
    r = x - hi.astype(F32)
    mid = _bf(r)
    lo = _bf(r - mid.astype(F32))
    return hi, mid, lo


def _mm3(sel, x):
    hi, mid, lo = _split3(x)
    return (jnp.dot(sel, hi, preferred_element_type=F32)
            + jnp.dot(sel, mid, preferred_element_type=F32)
            + jnp.dot(sel, lo, preferred_element_type=F32))


def _mm3_nt(sel, x):
    dn = (((1,), (1,)), ((), ()))
    hi, mid, lo = _split3(x)
    return (lax.dot_general(sel, hi, dn, preferred_element_type=F32)
            + lax.dot_general(sel, mid, dn, preferred_element_type=F32)
            + lax.dot_general(sel, lo, dn, preferred_element_type=F32))


def _sigmoid(x):
    return 1.0 / (1.0 + jnp.exp2(x * (-LOG2_E)))


def _silu(x):
    return x * _sigmoid(x)


def _log1pexp_neg_abs(x):
    return jnp.log(1.0 + jnp.exp(-jnp.abs(x)))


def _log_sigmoid(x):
    return jnp.minimum(x, 0.0) - _log1pexp_neg_abs(x)


def _softplus(x):
    return jnp.maximum(x, 0.0) + _log1pexp_neg_abs(x)


def _iota(shape, dim):
    return lax.broadcasted_iota(jnp.int32, shape, dim)


def _chunk_tril(n):
    t = _iota((n, n), 0)
    s = _iota((n, n), 1)
    return ((t >> CHUNK_SHIFT) == (s >> CHUNK_SHIFT)) & (s <= t)


def _head_rms(o):
    return o * lax.rsqrt(jnp.mean(o * o, axis=-1, keepdims=True) + EPS)


SUBLANES = 8
SUB_LEVELS = 2
CHUNK_TRIL = 2 * SUB_LEVELS


def _mixer_constants(tb):
    t = np.arange(tb)[:, None]
    r = np.arange(tb)[None, :]
    segm = []
    for lv in range(1, SUB_LEVELS + 1):
        same = (t >> lv) == (r >> lv)
        segm += [same & (r <= t), same & (r > t)]
    segm.append(((t >> CHUNK_SHIFT) == (r >> CHUNK_SHIFT)) & (r <= t))
    lane = np.arange(BRANCH_W)
    block = lambda n, w: (lane[:n, None] // w) == (lane[None, :n] // w)
    as_bf16 = lambda m: jnp.asarray(np.asarray(m, np.float32), BF16)
    return (as_bf16(np.stack(segm)),
            as_bf16(np.eye(16, LANES)),
            as_bf16(block(BRANCH_W, AT_HD)),
            as_bf16(block(LANES, AT_HD)),
            as_bf16(block(BRANCH_W, HEAD_W)))


def _inproj_kernel(x_ref, g_ref, wb_ref, wf_ref, zb_ref, zf_ref, h_ref):
    @pl.when(pl.program_id(1) == 0)
    def _():
        x = x_ref[...]
        ms = jnp.mean(x * x, axis=-1, keepdims=True)
        h = _bf((x * lax.rsqrt(ms + EPS)) * g_ref[...])
        h_ref[...] = h
        zf_ref[...] = jnp.dot(h, wf_ref[...], preferred_element_type=F32)

    zb_ref[...] = _bf(jnp.dot(h_ref[...], wb_ref[...], preferred_element_type=F32))


def _inproj(x2, g, wb, wf, tm, tn):
    n, d = x2.shape
    return pl.pallas_call(
        _inproj_kernel,
        grid=(n // tm, ZB_W // tn),
        in_specs=[
            pl.BlockSpec((tm, d), lambda i, j: (i, 0)),
            pl.BlockSpec((1, d), lambda i, j: (0, 0)),
            pl.BlockSpec((d, tn), lambda i, j: (0, j)),
            pl.BlockSpec((d, ZF_W), lambda i, j: (0, 0)),
        ],
        out_specs=[
            pl.BlockSpec((tm, tn), lambda i, j: (i, j)),
            pl.BlockSpec((tm, ZF_W), lambda i, j: (i, 0)),
        ],
        out_shape=[jax.ShapeDtypeStruct((n, ZB_W), BF16),
                   jax.ShapeDtypeStruct((n, ZF_W), F32)],
        scratch_shapes=[pltpu.VMEM((tm, d), BF16)],
        compiler_params=pltpu.CompilerParams(
            dimension_semantics=("arbitrary", "arbitrary"),
            vmem_limit_bytes=VMEM_LIMIT),
        name="inproj",
    )(x2, g, wb, wf)


def _hgrn_kernel(t_idx, q_ref, v_ref, g_ref, f_ref, lb_ref, og_ref, segm_ref, o_ref, st_ref, *, tb):
    @pl.when(t_idx == 0)
    def _():
        st_ref[...] = jnp.zeros(st_ref.shape, F32)

    n_chunks = tb // CHUNK
    z = f_ref[...]
    lb = lb_ref[...]
    sig = _sigmoid(z)
    f = lb + (1.0 - lb) * sig
    logf2 = jnp.log(jnp.maximum(f, TINY)) * LOG2_E
    k = (1.0 - lb) * (1.0 - sig)
    q = _silu(q_ref[...].astype(F32))

    hi = _bf(logf2)
    lo = _bf(logf2 - hi.astype(F32))

    def seg_sum(idx):
        sel = segm_ref[idx]
        return (jnp.dot(sel, hi, preferred_element_type=F32)
                + jnp.dot(sel, lo, preferred_element_type=F32))

    n_lv = CHUNK_SHIFT
    cum2 = seg_sum(CHUNK_TRIL)
    yield

    def rows_of(row_of_group):
        pieces = []
        for g in range(tb // SUBLANES):
            r = row_of_group(g)
            pieces.append(jnp.zeros((SUBLANES, BRANCH_W), F32) if r is None
                          else jnp.broadcast_to(cum2[r:r + 1, :], (SUBLANES, BRANCH_W)))
        return jnp.concatenate(pieces, axis=0)

    def seg_pair(lv):
        m = 1 << lv
        if m < SUBLANES:
            return seg_sum(2 * (lv - 1)), seg_sum(2 * (lv - 1) + 1)
        per = m // SUBLANES
        start = lambda g: (g // per) * m
        base = rows_of(lambda g: None if start(g) % CHUNK == 0 else start(g) - 1)
        top = rows_of(lambda g: start(g) + m - 1)
        return cum2 - base, top - cum2

    row = _iota((tb, 1), 0)
    qd, kd = [], []
    for lv in range(n_lv):
        upper = ((row >> lv) & 1) == 1
        if lv == 0:
            qd.append(_bf(q * jnp.exp2(jnp.where(upper, logf2, NEG_BIG))))
            kd.append(_bf(jnp.where(upper, 0.0, k)))
        else:
            prefix, suffix = seg_pair(lv)
            qd.append(_bf(q * jnp.exp2(jnp.where(upper, prefix, NEG_BIG))))
            yield
            kd.append(_bf(k * jnp.exp2(jnp.where(upper, NEG_BIG, suffix))))
        yield
    q_in = _bf(q * jnp.exp2(cum2))
    yield
    k_out = _bf(k * jnp.exp2(seg_pair(n_lv)[1]))
    qk = _bf(q * k)
    ones_b = jnp.ones((HEAD_W, CHUNK), BF16)
    t_i = _iota((CHUNK, CHUNK), 0)
    s_i = _iota((CHUNK, CHUNK), 1)
    block_masks = [(t_i >> (lv + 1)) == (s_i >> (lv + 1)) for lv in range(n_lv)]
    eye = t_i == s_i

    heads = range(N_HEADS)
    cs = [slice(h * HEAD_W, (h + 1) * HEAD_W) for h in heads]
    rs = [slice(ci * CHUNK, (ci + 1) * CHUNK) for ci in range(n_chunks)]
    cells = [(ci, h) for ci in range(n_chunks) for h in heads]
    a = {(ci, h): jnp.where(eye, jnp.dot(qk[rs[ci], cs[h]], ones_b, preferred_element_type=F32), 0.0)
         for ci, h in cells}
    yield
    for lv in range(n_lv):
        part = {(ci, h): _mm_nt(qd[lv][rs[ci], cs[h]], kd[lv][rs[ci], cs[h]]) for ci, h in cells}
        if lv < n_lv - 1:
            part = {c: jnp.where(block_masks[lv], part[c], 0.0) for c in cells}
        a = {c: a[c] + part[c] for c in cells}
        yield
    intra = {(ci, h): _mm(a[ci, h], v_ref[rs[ci], cs[h]]) for ci, h in cells}
    upd = {(ci, h): _mm_tn(v_ref[rs[ci], cs[h]], k_out[rs[ci], cs[h]]) for ci, h in cells}
    yield

    st = [st_ref[h] for h in heads]
    for ci in range(n_chunks):
        o = [intra[ci, h] + _mm_nt(q_in[rs[ci], cs[h]], st[h]) for h in heads]
        total = [cum2[(ci + 1) * CHUNK - 1:(ci + 1) * CHUNK, cs[h]] for h in heads]
        st = [jnp.exp2(total[h]) * st[h] + upd[ci, h] for h in heads]
        rms = [_head_rms(o[h]) for h in heads]
        for h in heads:
            out = rms[h] * og_ref[:, cs[h]] * _silu(g_ref[rs[ci], cs[h]].astype(F32))
            o_ref[rs[ci], cs[h]] = _bf(out)
        yield
    for h in heads:
        st_ref[h] = st[h]


def _mlstm_kernel(t_idx, qk_ref, v_ref, og_pre_ref, s1_ref, s2_ref, b0_ref, b1_ref, gain_ref,
                  segm_ref, eye_ref, o_ref, c_ref, m_ref, *, tb):
    @pl.when(t_idx == 0)
    def _():
        c_ref[...] = jnp.zeros(c_ref.shape, F32)
        m_ref[...] = jnp.zeros(m_ref.shape, F32)

    li_all = s1_ref[...] + b0_ref[...]
    lf_all = _log_sigmoid(s2_ref[...] + b1_ref[...])
    cum_all = _mm3(segm_ref[CHUNK_TRIL], lf_all)
    rr_all = li_all - cum_all
    eye = eye_ref[...]
    causal = _iota((CHUNK, CHUNK), 1) <= _iota((CHUNK, CHUNK), 0)
    lane = _iota((1, LANES), 1)
    ones_b = jnp.ones((CHUNK, LANES), BF16)
    ones_sq = jnp.ones((LANES, LANES), BF16)
    cum_bs = [jnp.broadcast_to(cum_all[:, h:h + 1], (tb, LANES)) for h in range(N_HEADS)]
    li_bs = [jnp.broadcast_to(li_all[:, h:h + 1], (tb, LANES)) for h in range(N_HEADS)]

    n_chunks = tb // CHUNK
    heads = range(N_HEADS)
    cells = [(ci, h) for ci in range(n_chunks) for h in heads]
    rs = [slice(ci * CHUNK, (ci + 1) * CHUNK) for ci in range(n_chunks)]
    vs = [slice(h * HEAD_W, (h + 1) * HEAD_W) for h in heads]
    halves = [(lane >> AT_HD_SHIFT) == i for i in range(2)]
    rr_t = [_mm3_nt(eye, rr_all[rs[ci]]) for ci in range(n_chunks)]
    yield
    qm, ks, v_aug, cum_b = {}, {}, {}, {}
    for ci, h in cells:
        gs = slice((h // 2) * LANES, (h // 2 + 1) * LANES)
        qm[ci, h] = jnp.where(halves[h % 2], qk_ref[rs[ci], gs], jnp.zeros((), BF16))
        ks[ci, h] = qk_ref[rs[ci], 2 * LANES + gs.start:2 * LANES + gs.stop].astype(F32) * (ML_DK ** -0.5)
        v_aug[ci, h] = jnp.concatenate([v_ref[rs[ci], vs[h]], ones_b], axis=1)
        cum_b[ci, h] = cum_bs[h][rs[ci]]
    logd = {c: jnp.where(causal, cum_b[c][:, :CHUNK] + rr_t[c[0]][c[1]:c[1] + 1, :], NEG_BIG)
            for c in cells}
    rowmax_b = {c: jnp.broadcast_to(jnp.max(logd[c], axis=-1, keepdims=True), (CHUNK, LANES))
                for c in cells}
    yield
    qk = {c: _mm_nt(qm[c], ks[c]) for c in cells}
    yield
    s0 = {c: _bf(qk[c] * jnp.exp(logd[c] - rowmax_b[c][:, :CHUNK])) for c in cells}
    yield
    p0 = {c: jnp.dot(s0[c], v_aug[c], preferred_element_type=F32) for c in cells}
    yield
    mref_b = {c: rowmax_b[c][CHUNK - 1:CHUNK, :] for c in cells}
    cum_last_b = {c: cum_b[c][CHUNK - 1:CHUNK, :] for c in cells}
    kw = {c: _bf(ks[c] * jnp.exp(cum_last_b[c] - cum_b[c] + li_bs[c[1]][rs[c[0]]] - mref_b[c]))
          for c in cells}
    yield
    upd = {c: _mm_tn(kw[c], v_aug[c]) for c in cells}
    yield

    cms = [c_ref[h] for h in heads]
    ms = [m_ref[h:h + 1, :] for h in heads]
    for ci in range(n_chunks):
        m_inter = [cum_b[ci, h] + ms[h] for h in heads]
        m_t = [jnp.maximum(m_inter[h], rowmax_b[ci, h]) for h in heads]
        qc = [_mm(qm[ci, h], cms[h]) for h in heads]
        e2 = [jnp.exp(rowmax_b[ci, h] - m_t[h]) for h in heads]
        wi = [jnp.exp(m_inter[h] - m_t[h]) for h in heads]
        m_new = [m_t[h][CHUNK - 1:CHUNK, :] for h in heads]
        dec = [jnp.exp(cum_last_b[ci, h] + ms[h] - m_new[h]) for h in heads]
        sc = [jnp.exp(mref_b[ci, h] - m_new[h]) for h in heads]
        cms = [jnp.concatenate([dec[h], dec[h]], axis=1) * cms[h]
               + jnp.concatenate([sc[h], sc[h]], axis=1) * upd[ci, h] for h in heads]
        ms = m_new
        yield
        num = [e2[h] * p0[ci, h][:, :HEAD_W] + wi[h] * qc[h][:, :HEAD_W] for h in heads]
        den = [e2[h] * p0[ci, h][:, HEAD_W:] + wi[h] * qc[h][:, HEAD_W:] for h in heads]
        hval = [num[h] / jnp.maximum(jnp.abs(den[h]), jnp.exp(-m_t[h])) for h in heads]
        mean_sq = [_mm(hval[h] * hval[h], ones_sq) * (1.0 / HEAD_W) for h in heads]
        for h in heads:
            out = (hval[h] * lax.rsqrt(mean_sq[h] + EPS) * gain_ref[:, vs[h]]
                   * _sigmoid(og_pre_ref[rs[ci], vs[h]].astype(F32)))
            o_ref[rs[ci], vs[h]] = _bf(out)
        yield
    for h in heads:
        c_ref[h] = cms[h]
        m_ref[h:h + 1, :] = ms[h]


def _swa_kernel(t_idx, q_ref, kvp_ref, kvc_ref, qg_ref, kg_ref, sink_ref, bias_ref, segq_ref,
                segk_ref, o_ref, *, n_win):
    first_cols = jnp.where(t_idx == 0, WINDOW, 0)

    qf = q_ref[...].astype(F32)
    ms_q = _mm(qf * qf, segq_ref[...]) * (1.0 / AT_HD)
    qn = _bf(qf * lax.rsqrt(ms_q + EPS) * qg_ref[...] * (AT_HD ** -0.5))

    kw = jnp.concatenate([kvp_ref[:, :LANES], kvc_ref[:, :LANES]], axis=0).astype(F32)
    vw = jnp.concatenate([kvp_ref[:, LANES:], kvc_ref[:, LANES:]], axis=0).astype(F32)
    ms_k = _mm(kw * kw, segk_ref[...]) * (1.0 / AT_HD)
    kn = kw * lax.rsqrt(ms_k + EPS) * kg_ref[...]
    lane = _iota((1, LANES), 1)
    halves = [(lane >> AT_HD_SHIFT) == i for i in range(2)]
    k_src = {True: _bf(kn), False: _bf(pltpu.roll(kn, AT_HD, 1))}
    v_roll = pltpu.roll(vw, AT_HD, 1)
    v_src = {(same, i): _bf(jnp.where(halves[i], vw if same else v_roll, 0.0))
             for same in (True, False) for i in range(2)}

    kpos = _iota((1, 2 * WINDOW), 1)
    yield
    heads = range(AT_HEADS)
    same = [(h % 2) == h // (AT_HEADS // 2) for h in heads]
    sink = [sink_ref[h:h + 1, 0:1] for h in heads]
    cells = [(w, h) for w in range(n_win) for h in heads]
    qs = [slice(w * WINDOW, (w + 1) * WINDOW) for w in range(n_win)]
    ws = [slice(w * WINDOW, (w + 2) * WINDOW) for w in range(n_win)]
    qh = {(w, h): jnp.where(halves[h % 2], qn[qs[w], (h // 2) * LANES:(h // 2 + 1) * LANES],
                            jnp.zeros((), BF16)) for w, h in cells}
    lg = {(w, h): _mm_nt(qh[w, h], k_src[same[h]][ws[w]]) + bias_ref[h] for w, h in cells}
    for h in heads:
        lg[0, h] = jnp.where(kpos < first_cols, NEG_BIG, lg[0, h])
    yield
    mx = {(w, h): jnp.maximum(jnp.max(lg[w, h], axis=-1, keepdims=True), sink[h]) for w, h in cells}
    yield
    p = {c: jnp.exp(lg[c] - mx[c]) for c in cells}
    yield
    denom = {(w, h): jnp.sum(p[w, h], axis=-1, keepdims=True) + jnp.exp(sink[h] - mx[w, h])
             for w, h in cells}
    pv = {(w, h): _mm(p[w, h], v_src[(same[h], h % 2)][ws[w]]) for w, h in cells}
    yield
    for w in range(n_win):
        for pair in range(AT_HEADS // 2):
            acc = (pv[w, 2 * pair] / denom[w, 2 * pair]
                   + pv[w, 2 * pair + 1] / denom[w, 2 * pair + 1])
            o_ref[qs[w], pair * LANES:(pair + 1) * LANES] = _bf(acc)
    yield


def _conv_silu(x_ref, prev_ref, w_ref, idx, has_prev):
    tb = x_ref.shape[0]
    x = x_ref[...].astype(F32)
    prev = prev_ref[CONV_TAIL - SUBLANES:, :].astype(F32) * has_prev
    x2 = jnp.concatenate([prev, x], axis=0)
    w = w_ref[:, idx * BRANCH_W:(idx + 1) * BRANCH_W]
    y = x * w[CONV_K - 1:CONV_K, :]
    for k in range(1, CONV_K):
        y = y + x2[SUBLANES - k:SUBLANES - k + tb, :] * w[CONV_K - 1 - k:CONV_K - k, :]
    return _silu(y)


def _dn_kernel(t_idx, q_ref, k_ref, v_ref, qp_ref, kp_ref, vp_ref, z_ref, s1_ref, s2_ref, w_ref,
               alog_ref, dt_ref, gain_ref, segm_ref, eye_ref, segh_ref, o_ref, st_ref, *, tb):
    @pl.when(t_idx == 0)
    def _():
        st_ref[...] = jnp.zeros(st_ref.shape, F32)

    n_chunks = tb // CHUNK
    has_prev = jnp.where(t_idx > 0, 1.0, 0.0)
    q = _conv_silu(q_ref, qp_ref, w_ref, 0, has_prev)
    yield
    k = _conv_silu(k_ref, kp_ref, w_ref, 1, has_prev)
    yield
    v = _conv_silu(v_ref, vp_ref, w_ref, 2, has_prev)
    yield
    seg = segh_ref[...]
    q = q * lax.rsqrt(_mm(q * q, seg) + EPS) * (DN_DK ** -0.5)
    k = k * lax.rsqrt(_mm(k * k, seg) + EPS)
    yield

    beta_all = _sigmoid(s1_ref[...])
    g_all = -jnp.exp(alog_ref[...]) * _softplus(s2_ref[...] + dt_ref[...])
    gam_all = _mm3(segm_ref[CHUNK_TRIL], g_all)
    gam_t = _mm3_nt(eye_ref[...], gam_all)
    tile = 2 * CHUNK
    n_tiles = tb // tile
    t_sq = _iota((tile, tile), 0)
    s_sq = _iota((tile, tile), 1)
    incl_t = _chunk_tril(tile)
    level_masks = []
    for lv in range(CHUNK_SHIFT):
        m = 1 << lv
        level_masks.append(((t_sq >> (lv + 1)) == (s_sq >> (lv + 1)))
                           & ((t_sq & m) != 0) & ((s_sq & m) == 0))

    heads = range(N_HEADS)
    cs = [slice(h * HEAD_W, (h + 1) * HEAD_W) for h in heads]
    gcol = {h: gam_all[:, N_HEADS + h:N_HEADS + h + 1] for h in heads}
    bcol = {h: beta_all[:, N_HEADS + h:N_HEADS + h + 1] for h in heads}
    q_dec = {h: q[:, cs[h]] * jnp.exp(gcol[h]) for h in heads}
    sol, attn = {}, {}
    yield

    def prep(ti):
        rt = slice(ti * tile, (ti + 1) * tile)
        decay = {h: jnp.exp(jnp.where(
            incl_t, gcol[h][rt] - gam_t[N_HEADS + h:N_HEADS + h + 1, rt], NEG_BIG)) for h in heads}
        kb = {h: _bf(k[rt, cs[h]]) for h in heads}
        low = {h: bcol[h][rt] * _mm_nt(kb[h], kb[h]) * decay[h] for h in heads}
        yield
        n_mat = {h: -jnp.where(level_masks[0], low[h], 0.0) for h in heads}
        for lm in level_masks[1:]:
            x_m = {}
            for h in heads:
                l_m = jnp.where(lm, low[h], 0.0)
                x_m[h] = l_m + _mm(l_m, n_mat[h])
            yield
            for h in heads:
                n_mat[h] = n_mat[h] - x_m[h] - _mm(n_mat[h], x_m[h])
            yield
        for h in heads:
            bc = bcol[h][rt]
            rhs = jnp.concatenate([v[rt, cs[h]] * bc,
                                   k[rt, cs[h]] * (bc * jnp.exp(gcol[h][rt]))], axis=1)
            sol[ti, h] = rhs + _mm(n_mat[h], rhs)
            attn[ti, h] = _bf(_mm_nt(q[rt, cs[h]], kb[h]) * decay[h])
        yield

    st = [st_ref[h] for h in heads]
    per_tile = tile // CHUNK
    vnew = {}

    def recur(ci):
        rs = slice(ci * CHUNK, (ci + 1) * CHUNK)
        ti, cj = divmod(ci, per_tile)
        rl = slice(cj * CHUNK, (cj + 1) * CHUNK)
        pad = [jnp.zeros((CHUNK, HEAD_W), BF16)] * (per_tile - cj - 1)
        if cj == 0:
            for h in heads:
                vnew[h] = []
        g_last = [gcol[h][ci * CHUNK + CHUNK - 1:(ci + 1) * CHUNK, :] for h in heads]
        k_dec = [_bf(k[rs, cs[h]] * jnp.exp(g_last[h] - gcol[h][rs])) for h in heads]
        ws = [_mm(jnp.concatenate([sol[ti, h][rl, HEAD_W:], q_dec[h][rs]], axis=0), st[h])
              for h in heads]
        v_new = [_bf(sol[ti, h][rl, :HEAD_W] - ws[h][:CHUNK]) for h in heads]
        for h in heads:
            vnew[h].append(v_new[h])
        o = [ws[h][CHUNK:] + jnp.dot(attn[ti, h][rl], jnp.concatenate(vnew[h] + pad, axis=0),
                                      preferred_element_type=F32) for h in heads]
        d_st = [_mm_tn(k_dec[h], v_new[h]) for h in heads]
        for h in heads:
            st[h] = jnp.exp(g_last[h]) * st[h] + d_st[h]
        rms = [_head_rms(o[h]) for h in heads]
        for h in heads:
            out = rms[h] * gain_ref[...] * _silu(z_ref[rs, cs[h]].astype(F32))
            o_ref[rs, cs[h]] = _bf(out)

    for _ in zip(*[prep(ti) for ti in range(n_tiles)]):
        yield
    for ci in range(n_chunks):
        recur(ci)
        yield
    for h in heads:
        st_ref[h] = st[h]


def _round_robin(gens):
    gens = list(gens)
    while gens:
        for g in list(gens):
            try:
                next(g)
            except StopIteration:
                gens.remove(g)


def _merge_body(o_scr, gate_refs, x_ref, wb_ref, wo_ref, y_ref):
    outs = [o_scr[:, n * BRANCH_W:(n + 1) * BRANCH_W] for n in range(N_BRANCH)]
    yield
    proj = [jnp.dot(outs[n], wb_ref[n], preferred_element_type=F32) for n in range(N_BRANCH)]
    yield
    gates = [_sigmoid(gate_refs[n][...].astype(F32)) for n in range(N_BRANCH)]
    yield
    merged = (gates[0] * proj[0] + gates[1] * proj[1]) + (gates[2] * proj[2] + gates[3] * proj[3])
    yield
    y_ref[...] = x_ref[...] + jnp.dot(_bf(merged), wo_ref[...], preferred_element_type=F32)
    yield


def _mixers_kernel(hq, hv, hg, hf, lb, og_h,
                   mqk, mv, mo, s1, s2, b0, b1, gain_m,
                   aq, kvp, kvc, qg, kg, sinks, bias,
                   dq, dk, dv, dqp, dkp, dvp, dz, conv_w, alog, dt, gain_d,
                   segm, eye16, seg64q, seg64k, seg128,
                   g0, g1, g2, g3, x_prev, wbr, wo,
                   y_ref, o_scr, st_h, c_m, m_m, st_d, *, tb, nt, total):
    s = pl.program_id(0)
    t_idx = lax.rem(jnp.minimum(s, total - 1), nt)

    @pl.when(s == 0)
    def _():
        o_scr[...] = jnp.zeros(o_scr.shape, BF16)

    at = lambda *refs: [r.at[0] for r in refs]
    o_a, o_b, o_c, o_d = (o_scr.at[:, n * BRANCH_W:(n + 1) * BRANCH_W] for n in range(N_BRANCH))
    _round_robin([
        _merge_body(o_scr, at(g0, g1, g2, g3), x_prev.at[0], wbr, wo, y_ref.at[0]),
        _dn_kernel(t_idx, *at(dq, dk, dv, dqp, dkp, dvp, dz, s1, s2), conv_w, alog, dt, gain_d,
                   segm, eye16, seg128, o_d, st_d, tb=tb),
        _hgrn_kernel(t_idx, *at(hq, hv, hg, hf), lb, og_h, segm, o_a, st_h, tb=tb),
        _mlstm_kernel(t_idx, *at(mqk, mv, mo, s1, s2), b0, b1, gain_m, segm, eye16,
                      o_b, c_m, m_m, tb=tb),
        _swa_kernel(t_idx, *at(aq, kvp, kvc), qg, kg, sinks, bias, seg64q, seg64k, o_c,
                    n_win=tb // WINDOW),
    ])


def _mixers(zb, zf, x3, lb, og_h, b0, b1, gain_m, qg, kg, sinks, bias, conv_w, alog, dt, gain_d,
            wbr, wo, batch, seq, tb):
    nt = seq // tb
    total = batch * nt
    cur = lambda s: jnp.minimum(s, total - 1)
    prv = lambda s: jnp.maximum(s - 1, 0)
    bt = lambda i: (i // nt, i % nt)
    blk = lambda c: pl.BlockSpec((1, tb, BRANCH_W), lambda s, c=c: (*bt(cur(s)), c // BRANCH_W))
    sm = lambda c: pl.BlockSpec((1, tb, LANES), lambda s, c=c: (*bt(cur(s)), c // LANES))
    full = lambda shape: pl.BlockSpec(shape, lambda s: (0,) * len(shape))
    kvw = 2 * LANES
    n_win = tb // WINDOW
    per = tb // CONV_TAIL
    kv_prev = pl.BlockSpec(
        (1, WINDOW, kvw),
        lambda s: (cur(s) // nt, jnp.maximum((cur(s) % nt) * n_win - 1, 0), ZB_AKV // kvw))
    dn_prev = lambda c: pl.BlockSpec(
        (1, CONV_TAIL, BRANCH_W),
        lambda s, c=c: (cur(s) // nt, jnp.maximum((cur(s) % nt) * per - 1, 0), c // BRANCH_W))
    gate = lambda k: pl.BlockSpec((1, tb, D_MODEL), lambda s, k=k: (*bt(prv(s)), ZB_GATE // D_MODEL + k))
    x_spec = pl.BlockSpec((1, tb, D_MODEL), lambda s: (*bt(prv(s)), 0))
    consts = _mixer_constants(tb)
    return pl.pallas_call(
        functools.partial(_mixers_kernel, tb=tb, nt=nt, total=total),
        grid=(total + 1,),
        in_specs=[
            blk(ZB_HQ), blk(ZB_HI), blk(ZB_HG), blk(ZF_HF), full((1, BRANCH_W)), full((1, BRANCH_W)),
            blk(ZB_MQK), blk(ZB_MV), blk(ZB_MO), sm(ZF_S1), sm(ZF_S2),
            full((1, LANES)), full((1, LANES)), full((1, BRANCH_W)),
            blk(ZB_AQ), kv_prev, pl.BlockSpec((1, tb, kvw), lambda s: (*bt(cur(s)), ZB_AKV // kvw)),
            full((1, BRANCH_W)), full((1, LANES)), full((AT_HEADS, LANES)),
            full((AT_HEADS, WINDOW, 2 * WINDOW)),
            blk(ZB_DQ), blk(ZB_DK), blk(ZB_DV), dn_prev(ZB_DQ), dn_prev(ZB_DK), dn_prev(ZB_DV),
            blk(ZB_DZ), full((CONV_K, 3 * BRANCH_W)), full((1, LANES)), full((1, LANES)),
            full((1, LANES)),
        ] + [full(c.shape) for c in consts] + [
            gate(0), gate(1), gate(2), gate(3), x_spec,
            full((N_BRANCH, BRANCH_W, D_MODEL)), full((D_MODEL, D_MODEL)),
        ],
        out_specs=x_spec,
        out_shape=jax.ShapeDtypeStruct((batch, seq, D_MODEL), F32),
        scratch_shapes=[pltpu.VMEM((tb, N_BRANCH * BRANCH_W), BF16),
                        pltpu.VMEM((N_HEADS, HEAD_W, HEAD_W), F32),
                        pltpu.VMEM((N_HEADS, LANES, 2 * HEAD_W), F32),
                        pltpu.VMEM((8, LANES), F32),
                        pltpu.VMEM((N_HEADS, DN_DK, HEAD_W), F32)],
        compiler_params=pltpu.CompilerParams(
            dimension_semantics=("arbitrary",),
            vmem_limit_bytes=VMEM_LIMIT),
        name="mixers",
    )(zb, zb, zb, zf, lb, og_h,
      zb, zb, zb, zf, zf, b0, b1, gain_m,
      zb, zb, zb, qg, kg, sinks, bias,
      zb, zb, zb, zb, zb, zb, zb, conv_w, alog, dt, gain_d, *consts,
      zb, zb, zb, zb, x3, wbr, wo)


def _mlp_kernel(x_ref, g_ref, wu_ref, wd_ref, y_ref, h_ref, acc_ref):
    j = pl.program_id(1)

    @pl.when(j == 0)
    def _():
        x = x_ref[...]
        ms = jnp.mean(x * x, axis=-1, keepdims=True)
        h_ref[...] = _bf((x * lax.rsqrt(ms + EPS)) * g_ref[...])
        acc_ref[...] = jnp.zeros(acc_ref.shape, F32)

    up = jnp.dot(h_ref[...], wu_ref[...], preferred_element_type=F32)
    act = jnp.square(jnp.maximum(up, 0.0))
    acc_ref[...] += jnp.dot(_bf(act), wd_ref[...], preferred_element_type=F32)

    @pl.when(j == pl.num_programs(1) - 1)
    def _():
        y_ref[...] = x_ref[...] + acc_ref[...]


def _mlp(x2, g, wu, wd, tm, tf):
    n = x2.shape[0]
    return pl.pallas_call(
        _mlp_kernel,
        grid=(n // tm, D_FF // tf),
        in_specs=[pl.BlockSpec((tm, D_MODEL), lambda i, j: (i, 0)),
                  pl.BlockSpec((1, D_MODEL), lambda i, j: (0, 0)),
                  pl.BlockSpec((D_MODEL, tf), lambda i, j: (0, j)),
                  pl.BlockSpec((tf, D_MODEL), lambda i, j: (j, 0))],
        out_specs=pl.BlockSpec((tm, D_MODEL), lambda i, j: (i, 0)),
        out_shape=jax.ShapeDtypeStruct((n, D_MODEL), F32),
        scratch_shapes=[pltpu.VMEM((tm, D_MODEL), BF16), pltpu.VMEM((tm, D_MODEL), F32)],
        compiler_params=pltpu.CompilerParams(
            dimension_semantics=("arbitrary", "arbitrary"), vmem_limit_bytes=VMEM_LIMIT),
        name="mlp",
    )(x2, g, wu, wd)


def _t5_bucket_table():
    n = np.arange(WINDOW)
    max_exact = N_BUCKETS // 2
    nf = np.maximum(n, max_exact).astype(np.float32)
    large = max_exact + (np.log(nf / np.float32(max_exact)) / np.float32(math.log(MAX_DISTANCE / max_exact))
                         * (N_BUCKETS - max_exact)).astype(np.int32)
    large = np.minimum(large, N_BUCKETS - 1)
    return np.where(n < max_exact, n, large)


def _swa_bias(rel_table):
    per_dist = rel_table.astype(F32)[_t5_bucket_table()].T
    n_heads = per_dist.shape[0]
    span = 3 * WINDOW
    pad_lo = jnp.full((n_heads, WINDOW - 1), NEG_BIG, F32)
    pad_hi = jnp.full((n_heads, span - 2 * WINDOW + 1), NEG_BIG, F32)
    v = jnp.concatenate([pad_lo, per_dist, pad_hi], axis=1)
    hank = jnp.tile(v, (1, WINDOW + 1))[:, :WINDOW * (span + 1)].reshape(n_heads, WINDOW, span + 1)
    return hank[:, :, :2 * WINDOW][:, :, ::-1]


def _lane_row(vals, offset):
    return jnp.zeros((1, LANES), F32).at[0, offset:offset + vals.shape[0]].set(vals.astype(F32))


def _layout_w_in(w):
    sizes = (512, 512, 512, 512, 256, 256, 512, 4, 4, 512, 512, 128, 128, 1536, 4, 4, 512, 4096)
    parts, start = [], 0
    for s in sizes:
        parts.append(_bf(w[:, start:start + s]))
        start += s
    (hq, hf, hi, hg, mq, mk, mv, mi, mf, mo, aq, ak, av, dqkv, db, da, dz, gate) = parts
    zpad = lambda n: jnp.zeros((w.shape[0], n), BF16)
    wb = jnp.concatenate([hq, hi, hg, mq, mk, mv, mo, aq, ak, av, zpad(256), dqkv, dz, gate], axis=1)
    wf = jnp.concatenate([hf, mi, db, zpad(LANES - 8), mf, da, zpad(LANES - 8)], axis=1)
    return wb, wf


def _tiles(n, seq):
    tiles = dict(
        inproj_rows=1024, inproj_cols=2560,
        mixer_rows=256,
        mlp_rows=512, mlp_cols=D_FF,
    )
    assert n % tiles["inproj_rows"] == 0 and ZB_W % tiles["inproj_cols"] == 0
    assert seq % tiles["mixer_rows"] == 0 and n % tiles["mlp_rows"] == 0
    return tiles


def kernel(x, norm_mix_g, w_in, hgrn_lb_table, hgrn_out_g, mlstm_if_bias, mlstm_out_g,
           attn_q_norm_g, attn_k_norm_g, attn_sinks, rel_bias_table, dn_conv_w, dn_a_log,
           dn_dt_bias, dn_out_g, w_branch, w_out, norm_mlp_g, w_up, w_down):
    batch, seq, d = x.shape
    depth = w_in.shape[0]
    n = batch * seq
    tiles = _tiles(n, seq)
    assert d == D_MODEL

    lb_p = jax.nn.softmax(hgrn_lb_table.astype(F32), axis=0)
    lower_bounds = jnp.cumsum(lb_p, axis=0) - lb_p[0]
    bias = _swa_bias(rel_bias_table)

    x2 = x.reshape(n, d)
    for l in range(depth):
        wb, wf = _layout_w_in(w_in[l])
        zb, zf = _inproj(x2, norm_mix_g[l].reshape(1, d), wb, wf, tiles["inproj_rows"], tiles["inproj_cols"])

        x2 = _mixers(
            zb.reshape(batch, seq, ZB_W), zf.reshape(batch, seq, ZF_W), x2.reshape(batch, seq, d),
            lower_bounds[l].reshape(1, -1), hgrn_out_g[l].reshape(1, -1).astype(F32),
            _lane_row(mlstm_if_bias[l, 0], 0), _lane_row(mlstm_if_bias[l, 1], 0),
            mlstm_out_g[l].reshape(1, -1).astype(F32),
            jnp.tile(attn_q_norm_g[l].astype(F32), AT_HEADS).reshape(1, -1),
            jnp.tile(attn_k_norm_g[l].astype(F32), 2).reshape(1, -1),
            jnp.broadcast_to(attn_sinks[l].astype(F32)[:, None], (AT_HEADS, LANES)), bias,
            dn_conv_w[l].astype(F32), _lane_row(dn_a_log[l], N_HEADS),
            _lane_row(dn_dt_bias[l], N_HEADS), dn_out_g[l].reshape(1, -1).astype(F32),
            _bf(w_branch[l]), _bf(w_out[l]), batch, seq, tiles["mixer_rows"]).reshape(n, d)
        x2 = _mlp(x2, norm_mlp_g[l].reshape(1, d), _bf(w_up[l]), _bf(w_down[l]),
                  tiles["mlp_rows"], tiles["mlp_cols"])
    return x2.reshape(batch, seq, d)
```

```python
import functools
import math

import numpy as np
import jax
import jax.numpy as jnp
from jax import lax
from jax.experimental import pallas as pl
from jax.experimental.pallas import tpu as pltpu

F32 = jnp.float32
BF16 = jnp.bfloat16

D_MODEL = 1024
EPS = 1e-6
CHUNK = 64
NEG_BIG = -1e30
TINY = 1e-30
LANES = 128
HEAD_W = 128
N_HEADS = 4
BRANCH_W = 512
N_BRANCH = 4
D_FF = 4 * D_MODEL

AT_HEADS = 8
AT_HD = 64
WINDOW = 128
N_BUCKETS = 32
MAX_DISTANCE = 128
CONV_K = 4
ML_DK = 64
DN_DK = 128
CHUNK_SHIFT = 6
AT_HD_SHIFT = 6
LOG2_E = 1.4426950408889634
CONV_TAIL = 16

ZB_HQ, ZB_HI, ZB_HG = 0, 512, 1024
ZB_MQK, ZB_MV, ZB_MO = 1536, 2048, 2560
ZB_AQ, ZB_AKV = 3072, 3584
ZB_DQ, ZB_DK, ZB_DV, ZB_DZ = 4096, 4608, 5120, 5632
ZB_GATE = 6144
ZB_W = 10240
ZF_HF, ZF_S1, ZF_S2 = 0, 512, 640
ZF_W = 768

VMEM_LIMIT = 56 * 1024 * 1024


def _bf(x):
    return x.astype(BF16)


def _mm(a, b):
    return jnp.dot(_bf(a), _bf(b), preferred_element_type=F32)


def _mm_nt(a, b):
    return lax.dot_general(_bf(a), _bf(b), (((1,), (1,)), ((), ())),
                           preferred_element_type=F32)


def _mm_tn(a, b):
    return lax.dot_general(_bf(a), _bf(b), (((0,), (0,)), ((), ())),
                           preferred_element_type=F32)


def _split3(x):
    hi = _bf(x)
    r = x - hi.astype(F32)
    mid = _bf(r)
    lo = _bf(r - mid.astype(F32))
    return hi, mid, lo


def _mm3(sel, x):
    hi, mid, lo = _split3(x)
    return (jnp.dot(sel, hi, preferred_element_type=F32)
            + jnp.dot(sel, mid, preferred_element_type=F32)
            + jnp.dot(sel, lo, preferred_element_type=F32))


def _mm3_nt(sel, x):
    dn = (((1,), (1,)), ((), ()))
    hi, mid, lo = _split3(x)
    return (lax.dot_general(sel, hi, dn, preferred_element_type=F32)
            + lax.dot_general(sel, mid, dn, preferred_element_type=F32)
            + lax.dot_general(sel, lo, dn, preferred_element_type=F32))


def _sigmoid(x):
    return 1.0 / (1.0 + jnp.exp2(x * (-LOG2_E)))


def _silu(x):
    return x * _sigmoid(x)


def _log1pexp_neg_abs(x):
    return jnp.log(1.0 + jnp.exp(-jnp.abs(x)))


def _log_sigmoid(x):
    return jnp.minimum(x, 0.0) - _log1pexp_neg_abs(x)


def _softplus(x):
    return jnp.maximum(x, 0.0) + _log1pexp_neg_abs(x)


def _iota(shape, dim):
    return lax.broadcasted_iota(jnp.int32, shape, dim)


def _chunk_tril(n):
    t = _iota((n, n), 0)
    s = _iota((n, n), 1)
    return ((t >> CHUNK_SHIFT) == (s >> CHUNK_SHIFT)) & (s <= t)


def _head_rms(o):
    return o * lax.rsqrt(jnp.mean(o * o, axis=-1, keepdims=True) + EPS)


SUBLANES = 8
SUB_LEVELS = 2
CHUNK_TRIL = 2 * SUB_LEVELS


def _mixer_constants(tb):
    t = np.arange(tb)[:, None]
    r = np.arange(tb)[None, :]
    segm = []
    for lv in range(1, SUB_LEVELS + 1):
        same = (t >> lv) == (r >> lv)
        segm += [same & (r <= t), same & (r > t)]
    segm.append(((t >> CHUNK_SHIFT) == (r >> CHUNK_SHIFT)) & (r <= t))
    lane = np.arange(BRANCH_W)
    block = lambda n, w: (lane[:n, None] // w) == (lane[None, :n] // w)
    as_bf16 = lambda m: jnp.asarray(np.asarray(m, np.float32), BF16)
    return (as_bf16(np.stack(segm)),
            as_bf16(np.eye(16, LANES)),
            as_bf16(block(BRANCH_W, AT_HD)),
            as_bf16(block(LANES, AT_HD)),
            as_bf16(block(BRANCH_W, HEAD_W)))


def _inproj_kernel(x_ref, g_ref, wb_ref, wf_ref, zb_ref, zf_ref, h_ref):
    @pl.when(pl.program_id(1) == 0)
    def _():
        x = x_ref[...]
        ms = jnp.mean(x * x, axis=-1, keepdims=True)
        h = _bf((x * lax.rsqrt(ms + EPS)) * g_ref[...])
        h_ref[...] = h
        zf_ref[...] = jnp.dot(h, wf_ref[...], preferred_element_type=F32)

    zb_ref[...] = _bf(jnp.dot(h_ref[...], wb_ref[...], preferred_element_type=F32))


def _inproj(x2, g, wb, wf, tm, tn):
    n, d = x2.shape
    return pl.pallas_call(
        _inproj_kernel,
        grid=(n // tm, ZB_W // tn),
        in_specs=[
            pl.BlockSpec((tm, d), lambda i, j: (i, 0)),
            pl.BlockSpec((1, d), lambda i, j: (0, 0)),
            pl.BlockSpec((d, tn), lambda i, j: (0, j)),
            pl.BlockSpec((d, ZF_W), lambda i, j: (0, 0)),
        ],
        out_specs=[
            pl.BlockSpec((tm, tn), lambda i, j: (i, j)),
            pl.BlockSpec((tm, ZF_W), lambda i, j: (i, 0)),
        ],
        out_shape=[jax.ShapeDtypeStruct((n, ZB_W), BF16),
                   jax.ShapeDtypeStruct((n, ZF_W), F32)],
        scratch_shapes=[pltpu.VMEM((tm, d), BF16)],
        compiler_params=pltpu.CompilerParams(
            dimension_semantics=("arbitrary", "arbitrary"),
            vmem_limit_bytes=VMEM_LIMIT),
        name="inproj",
    )(x2, g, wb, wf)


def _hgrn_kernel(t_idx, q_ref, v_ref, g_ref, f_ref, lb_ref, og_ref, segm_ref, o_ref, st_ref, *, tb):
    @pl.when(t_idx == 0)
    def _():
        st_ref[...] = jnp.zeros(st_ref.shape, F32)

    n_chunks = tb // CHUNK
    z = f_ref[...]
    lb = lb_ref[...]
    sig = _sigmoid(z)
    f = lb + (1.0 - lb) * sig
    logf2 = jnp.log(jnp.maximum(f, TINY)) * LOG2_E
    k = (1.0 - lb) * (1.0 - sig)
    q = _silu(q_ref[...].astype(F32))

    hi = _bf(logf2)
    lo = _bf(logf2 - hi.astype(F32))

    def seg_sum(idx):
        sel = segm_ref[idx]
        return (jnp.dot(sel, hi, preferred_element_type=F32)
                + jnp.dot(sel, lo, preferred_element_type=F32))

    n_lv = CHUNK_SHIFT
    cum2 = seg_sum(CHUNK_TRIL)
    yield

    def rows_of(row_of_group):
        pieces = []
        for g in range(tb // SUBLANES):
            r = row_of_group(g)
            pieces.append(jnp.zeros((SUBLANES, BRANCH_W), F32) if r is None
                          else jnp.broadcast_to(cum2[r:r + 1, :], (SUBLANES, BRANCH_W)))
        return jnp.concatenate(pieces, axis=0)

    def seg_pair(lv):
        m = 1 << lv
        if m < SUBLANES:
            return seg_sum(2 * (lv - 1)), seg_sum(2 * (lv - 1) + 1)
        per = m // SUBLANES
        start = lambda g: (g // per) * m
        base = rows_of(lambda g: None if start(g) % CHUNK == 0 else start(g) - 1)
        top = rows_of(lambda g: start(g) + m - 1)
        return cum2 - base, top - cum2

    row = _iota((tb, 1), 0)
    qd, kd = [], []
    for lv in range(n_lv):
        upper = ((row >> lv) & 1) == 1
        if lv == 0:
            qd.append(_bf(q * jnp.exp2(jnp.where(upper, logf2, NEG_BIG))))
            kd.append(_bf(jnp.where(upper, 0.0, k)))
        else:
            prefix, suffix = seg_pair(lv)
            qd.append(_bf(q * jnp.exp2(jnp.where(upper, prefix, NEG_BIG))))
            yield
            kd.append(_bf(k * jnp.exp2(jnp.where(upper, NEG_BIG, suffix))))
        yield
    q_in = _bf(q * jnp.exp2(cum2))
    yield
    k_out = _bf(k * jnp.exp2(seg_pair(n_lv)[1]))
    qk = _bf(q * k)
    ones_b = jnp.ones((HEAD_W, CHUNK), BF16)
    t_i = _iota((CHUNK, CHUNK), 0)
    s_i = _iota((CHUNK, CHUNK), 1)
    block_masks = [(t_i >> (lv + 1)) == (s_i >> (lv + 1)) for lv in range(n_lv)]
    eye = t_i == s_i

    heads = range(N_HEADS)
    cs = [slice(h * HEAD_W, (h + 1) * HEAD_W) for h in heads]
    rs = [slice(ci * CHUNK, (ci + 1) * CHUNK) for ci in range(n_chunks)]
    cells = [(ci, h) for ci in range(n_chunks) for h in heads]
    a = {(ci, h): jnp.where(eye, jnp.dot(qk[rs[ci], cs[h]], ones_b, preferred_element_type=F32), 0.0)
         for ci, h in cells}
    yield
    for lv in range(n_lv):
        part = {(ci, h): _mm_nt(qd[lv][rs[ci], cs[h]], kd[lv][rs[ci], cs[h]]) for ci, h in cells}
        if lv < n_lv - 1:
            part = {c: jnp.where(block_masks[lv], part[c], 0.0) for c in cells}
        a = {c: a[c] + part[c] for c in cells}
        yield
    intra = {(ci, h): _mm(a[ci, h], v_ref[rs[ci], cs[h]]) for ci, h in cells}
    upd = {(ci, h): _mm_tn(v_ref[rs[ci], cs[h]], k_out[rs[ci], cs[h]]) for ci, h in cells}
    yield

    st = [st_ref[h] for h in heads]
    for ci in range(n_chunks):
        o = [intra[ci, h] + _mm_nt(q_in[rs[ci], cs[h]], st[h]) for h in heads]
        total = [cum2[(ci + 1) * CHUNK - 1:(ci + 1) * CHUNK, cs[h]] for h in heads]
        st = [jnp.exp2(total[h]) * st[h] + upd[ci, h] for h in heads]
        rms = [_head_rms(o[h]) for h in heads]
        for h in heads:
            out = rms[h] * og_ref[:, cs[h]] * _silu(g_ref[rs[ci], cs[h]].astype(F32))
            o_ref[rs[ci], cs[h]] = _bf(out)
        yield
    for h in heads:
        st_ref[h] = st[h]


def _mlstm_kernel(t_idx, qk_ref, v_ref, og_pre_ref, s1_ref, s2_ref, b0_ref, b1_ref, gain_ref,
                  segm_ref, eye_ref, o_ref, c_ref, m_ref, *, tb):
    @pl.when(t_idx == 0)
    def _():
        c_ref[...] = jnp.zeros(c_ref.shape, F32)
        m_ref[...] = jnp.zeros(m_ref.shape, F32)

    li_all = s1_ref[...] + b0_ref[...]
    lf_all = _log_sigmoid(s2_ref[...] + b1_ref[...])
    cum_all = _mm3(segm_ref[CHUNK_TRIL], lf_all)
    rr_all = li_all - cum_all
    eye = eye_ref[...]
    causal = _iota((CHUNK, CHUNK), 1) <= _iota((CHUNK, CHUNK), 0)
    lane = _iota((1, LANES), 1)
    ones_b = jnp.ones((CHUNK, LANES), BF16)
    ones_sq = jnp.ones((LANES, LANES), BF16)
    cum_bs = [jnp.broadcast_to(cum_all[:, h:h + 1], (tb, LANES)) for h in range(N_HEADS)]
    li_bs = [jnp.broadcast_to(li_all[:, h:h + 1], (tb, LANES)) for h in range(N_HEADS)]

    n_chunks = tb // CHUNK
    heads = range(N_HEADS)
    cells = [(ci, h) for ci in range(n_chunks) for h in heads]
    rs = [slice(ci * CHUNK, (ci + 1) * CHUNK) for ci in range(n_chunks)]
    vs = [slice(h * HEAD_W, (h + 1) * HEAD_W) for h in heads]
    halves = [(lane >> AT_HD_SHIFT) == i for i in range(2)]
    rr_t = [_mm3_nt(eye, rr_all[rs[ci]]) for ci in range(n_chunks)]
    yield
    qm, ks, v_aug, cum_b = {}, {}, {}, {}
    for ci, h in cells:
        gs = slice((h // 2) * LANES, (h // 2 + 1) * LANES)
        qm[ci, h] = jnp.where(halves[h % 2], qk_ref[rs[ci], gs], jnp.zeros((), BF16))
        ks[ci, h] = qk_ref[rs[ci], 2 * LANES + gs.start:2 * LANES + gs.stop].astype(F32) * (ML_DK ** -0.5)
        v_aug[ci, h] = jnp.concatenate([v_ref[rs[ci], vs[h]], ones_b], axis=1)
        cum_b[ci, h] = cum_bs[h][rs[ci]]
    logd = {c: jnp.where(causal, cum_b[c][:, :CHUNK] + rr_t[c[0]][c[1]:c[1] + 1, :], NEG_BIG)
            for c in cells}
    rowmax_b = {c: jnp.broadcast_to(jnp.max(logd[c], axis=-1, keepdims=True), (CHUNK, LANES))
                for c in cells}
    yield
    qk = {c: _mm_nt(qm[c], ks[c]) for c in cells}
    yield
    s0 = {c: _bf(qk[c] * jnp.exp(logd[c] - rowmax_b[c][:, :CHUNK])) for c in cells}
    yield
    p0 = {c: jnp.dot(s0[c], v_aug[c], preferred_element_type=F32) for c in cells}
    yield
    mref_b = {c: rowmax_b[c][CHUNK - 1:CHUNK, :] for c in cells}
    cum_last_b = {c: cum_b[c][CHUNK - 1:CHUNK, :] for c in cells}
    kw = {c: _bf(ks[c] * jnp.exp(cum_last_b[c] - cum_b[c] + li_bs[c[1]][rs[c[0]]] - mref_b[c]))
          for c in cells}
    yield
    upd = {c: _mm_tn(kw[c], v_aug[c]) for c in cells}
    yield

    cms = [c_ref[h] for h in heads]
    ms = [m_ref[h:h + 1, :] for h in heads]
    for ci in range(n_chunks):
        m_inter = [cum_b[ci, h] + ms[h] for h in heads]
        m_t = [jnp.maximum(m_inter[h], rowmax_b[ci, h]) for h in heads]
        qc = [_mm(qm[ci, h], cms[h]) for h in heads]
        e2 = [jnp.exp(rowmax_b[ci, h] - m_t[h]) for h in heads]
        wi = [jnp.exp(m_inter[h] - m_t[h]) for h in heads]
        m_new = [m_t[h][CHUNK - 1:CHUNK, :] for h in heads]
        dec = [jnp.exp(cum_last_b[ci, h] + ms[h] - m_new[h]) for h in heads]
        sc = [jnp.exp(mref_b[ci, h] - m_new[h]) for h in heads]
        cms = [jnp.concatenate([dec[h], dec[h]], axis=1) * cms[h]
               + jnp.concatenate([sc[h], sc[h]], axis=1) * upd[ci, h] for h in heads]
        ms = m_new
        yield
        num = [e2[h] * p0[ci, h][:, :HEAD_W] + wi[h] * qc[h][:, :HEAD_W] for h in heads]
        den = [e2[h] * p0[ci, h][:, HEAD_W:] + wi[h] * qc[h][:, HEAD_W:] for h in heads]
        hval = [num[h] / jnp.maximum(jnp.abs(den[h]), jnp.exp(-m_t[h])) for h in heads]
        mean_sq = [_mm(hval[h] * hval[h], ones_sq) * (1.0 / HEAD_W) for h in heads]
        for h in heads:
            out = (hval[h] * lax.rsqrt(mean_sq[h] + EPS) * gain_ref[:, vs[h]]
                   * _sigmoid(og_pre_ref[rs[ci], vs[h]].astype(F32)))
            o_ref[rs[ci], vs[h]] = _bf(out)
        yield
    for h in heads:
        c_ref[h] = cms[h]
        m_ref[h:h + 1, :] = ms[h]


def _swa_kernel(t_idx, q_ref, kvp_ref, kvc_ref, qg_ref, kg_ref, sink_ref, bias_ref, segq_ref,
                segk_ref, o_ref, *, n_win):
    first_cols = jnp.where(t_idx == 0, WINDOW, 0)

    qf = q_ref[...].astype(F32)
    ms_q = _mm(qf * qf, segq_ref[...]) * (1.0 / AT_HD)
    qn = _bf(qf * lax.rsqrt(ms_q + EPS) * qg_ref[...] * (AT_HD ** -0.5))

    kw = jnp.concatenate([kvp_ref[:, :LANES], kvc_ref[:, :LANES]], axis=0).astype(F32)
    vw = jnp.concatenate([kvp_ref[:, LANES:], kvc_ref[:, LANES:]], axis=0).astype(F32)
    ms_k = _mm(kw * kw, segk_ref[...]) * (1.0 / AT_HD)
    kn = kw * lax.rsqrt(ms_k + EPS) * kg_ref[...]
    lane = _iota((1, LANES), 1)
    halves = [(lane >> AT_HD_SHIFT) == i for i in range(2)]
    k_src = {True: _bf(kn), False: _bf(pltpu.roll(kn, AT_HD, 1))}
    v_roll = pltpu.roll(vw, AT_HD, 1)
    v_src = {(same, i): _bf(jnp.where(halves[i], vw if same else v_roll, 0.0))
             for same in (True, False) for i in range(2)}

    kpos = _iota((1, 2 * WINDOW), 1)
    yield
    heads = range(AT_HEADS)
    same = [(h % 2) == h // (AT_HEADS // 2) for h in heads]
    sink = [sink_ref[h:h + 1, 0:1] for h in heads]
    cells = [(w, h) for w in range(n_win) for h in heads]
    qs = [slice(w * WINDOW, (w + 1) * WINDOW) for w in range(n_win)]
    ws = [slice(w * WINDOW, (w + 2) * WINDOW) for w in range(n_win)]
    qh = {(w, h): jnp.where(halves[h % 2], qn[qs[w], (h // 2) * LANES:(h // 2 + 1) * LANES],
                            jnp.zeros((), BF16)) for w, h in cells}
    lg = {(w, h): _mm_nt(qh[w, h], k_src[same[h]][ws[w]]) + bias_ref[h] for w, h in cells}
    for h in heads:
        lg[0, h] = jnp.where(kpos < first_cols, NEG_BIG, lg[0, h])
    yield
    mx = {(w, h): jnp.maximum(jnp.max(lg[w, h], axis=-1, keepdims=True), sink[h]) for w, h in cells}
    yield
    p = {c: jnp.exp(lg[c] - mx[c]) for c in cells}
    yield
    denom = {(w, h): jnp.sum(p[w, h], axis=-1, keepdims=True) + jnp.exp(sink[h] - mx[w, h])
             for w, h in cells}
    pv = {(w, h): _mm(p[w, h], v_src[(same[h], h % 2)][ws[w]]) for w, h in cells}
    yield
    for w in range(n_win):
        for pair in range(AT_HEADS // 2):
            acc = (pv[w, 2 * pair] / denom[w, 2 * pair]
                   + pv[w, 2 * pair + 1] / denom[w, 2 * pair + 1])
            o_ref[qs[w], pair * LANES:(pair + 1) * LANES] = _bf(acc)
    yield


def _conv_silu(x_ref, prev_ref, w_ref, idx, has_prev):
    tb = x_ref.shape[0]
    x = x_ref[...].astype(F32)
    prev = prev_ref[CONV_TAIL - SUBLANES:, :].astype(F32) * has_prev
    x2 = jnp.concatenate([prev, x], axis=0)
    w = w_ref[:, idx * BRANCH_W:(idx + 1) * BRANCH_W]
    y = x * w[CONV_K - 1:CONV_K, :]
    for k in range(1, CONV_K):
        y = y + x2[SUBLANES - k:SUBLANES - k + tb, :] * w[CONV_K - 1 - k:CONV_K - k, :]
    return _silu(y)


def _dn_kernel(t_idx, q_ref, k_ref, v_ref, qp_ref, kp_ref, vp_ref, z_ref, s1_ref, s2_ref, w_ref,
               alog_ref, dt_ref, gain_ref, segm_ref, eye_ref, segh_ref, o_ref, st_ref, *, tb):
    @pl.when(t_idx == 0)
    def _():
        st_ref[...] = jnp.zeros(st_ref.shape, F32)

    n_chunks = tb // CHUNK
    has_prev = jnp.where(t_idx > 0, 1.0, 0.0)
    q = _conv_silu(q_ref, qp_ref, w_ref, 0, has_prev)
    yield
    k = _conv_silu(k_ref, kp_ref, w_ref, 1, has_prev)
    yield
    v = _conv_silu(v_ref, vp_ref, w_ref, 2, has_prev)
    yield
    seg = segh_ref[...]
    q = q * lax.rsqrt(_mm(q * q, seg) + EPS) * (DN_DK ** -0.5)
    k = k * lax.rsqrt(_mm(k * k, seg) + EPS)
    yield

    beta_all = _sigmoid(s1_ref[...])
    g_all = -jnp.exp(alog_ref[...]) * _softplus(s2_ref[...] + dt_ref[...])
    gam_all = _mm3(segm_ref[CHUNK_TRIL], g_all)
    gam_t = _mm3_nt(eye_ref[...], gam_all)
    tile = 2 * CHUNK
    n_tiles = tb // tile
    t_sq = _iota((tile, tile), 0)
    s_sq = _iota((tile, tile), 1)
    incl_t = _chunk_tril(tile)
    level_masks = []
    for lv in range(CHUNK_SHIFT):
        m = 1 << lv
        level_masks.append(((t_sq >> (lv + 1)) == (s_sq >> (lv + 1)))
                           & ((t_sq & m) != 0) & ((s_sq & m) == 0))

    heads = range(N_HEADS)
    cs = [slice(h * HEAD_W, (h + 1) * HEAD_W) for h in heads]
    gcol = {h: gam_all[:, N_HEADS + h:N_HEADS + h + 1] for h in heads}
    bcol = {h: beta_all[:, N_HEADS + h:N_HEADS + h + 1] for h in heads}
    q_dec = {h: q[:, cs[h]] * jnp.exp(gcol[h]) for h in heads}
    sol, attn = {}, {}
    yield

    def prep(ti):
        rt = slice(ti * tile, (ti + 1) * tile)
        decay = {h: jnp.exp(jnp.where(
            incl_t, gcol[h][rt] - gam_t[N_HEADS + h:N_HEADS + h + 1, rt], NEG_BIG)) for h in heads}
        kb = {h: _bf(k[rt, cs[h]]) for h in heads}
        low = {h: bcol[h][rt] * _mm_nt(kb[h], kb[h]) * decay[h] for h in heads}
        yield
        n_mat = {h: -jnp.where(level_masks[0], low[h], 0.0) for h in heads}
        for lm in level_masks[1:]:
            l_m = {h: jnp.where(lm, low[h], 0.0) for h in heads}
            ln = {h: _mm(l_m[h], n_mat[h]) for h in heads}
            x_m = {h: l_m[h] + ln[h] for h in heads}
            yield
            nx = {h: _mm(n_mat[h], x_m[h]) for h in heads}
            n_mat = {h: n_mat[h] - x_m[h] - nx[h] for h in heads}
            yield
        rhs = {h: jnp.concatenate([v[rt, cs[h]] * bcol[h][rt],
                                   k[rt, cs[h]] * (bcol[h][rt] * jnp.exp(gcol[h][rt]))], axis=1)
               for h in heads}
        n_rhs = {h: _mm(n_mat[h], rhs[h]) for h in heads}
        qk = {h: _mm_nt(q[rt, cs[h]], kb[h]) for h in heads}
        for h in heads:
            sol[ti, h] = rhs[h] + n_rhs[h]
            attn[ti, h] = _bf(qk[h] * decay[h])
        yield

    st = [st_ref[h] for h in heads]
    per_tile = tile // CHUNK
    vnew = {}

    def recur(ci):
        rs = slice(ci * CHUNK, (ci + 1) * CHUNK)
        ti, cj = divmod(ci, per_tile)
        rl = slice(cj * CHUNK, (cj + 1) * CHUNK)
        pad = [jnp.zeros((CHUNK, HEAD_W), BF16)] * (per_tile - cj - 1)
        if cj == 0:
            for h in heads:
                vnew[h] = []
        g_last = [gcol[h][ci * CHUNK + CHUNK - 1:(ci + 1) * CHUNK, :] for h in heads]
        k_dec = [_bf(k[rs, cs[h]] * jnp.exp(g_last[h] - gcol[h][rs])) for h in heads]
        ws = [_mm(jnp.concatenate([sol[ti, h][rl, HEAD_W:], q_dec[h][rs]], axis=0), st[h])
              for h in heads]
        v_new = [_bf(sol[ti, h][rl, :HEAD_W] - ws[h][:CHUNK]) for h in heads]
        for h in heads:
            vnew[h].append(v_new[h])
        o = [ws[h][CHUNK:] + jnp.dot(attn[ti, h][rl], jnp.concatenate(vnew[h] + pad, axis=0),
                                      preferred_element_type=F32) for h in heads]
        d_st = [_mm_tn(k_dec[h], v_new[h]) for h in heads]
        for h in heads:
            st[h] = jnp.exp(g_last[h]) * st[h] + d_st[h]
        rms = [_head_rms(o[h]) for h in heads]
        for h in heads:
            out = rms[h] * gain_ref[...] * _silu(z_ref[rs, cs[h]].astype(F32))
            o_ref[rs, cs[h]] = _bf(out)

    for _ in zip(*[prep(ti) for ti in range(n_tiles)]):
        yield
    for ci in range(n_chunks):
        recur(ci)
        yield
    for h in heads:
        st_ref[h] = st[h]


def _round_robin(gens):
    gens = list(gens)
    while gens:
        for g in list(gens):
            try:
                next(g)
            except StopIteration:
                gens.remove(g)


def _merge_body(o_scr, gate_refs, x_ref, wb_ref, wo_ref, y_ref):
    outs = [o_scr[:, n * BRANCH_W:(n + 1) * BRANCH_W] for n in range(N_BRANCH)]
    yield
    proj = [jnp.dot(outs[n], wb_ref[n], preferred_element_type=F32) for n in range(N_BRANCH)]
    yield
    gates = [_sigmoid(gate_refs[n][...].astype(F32)) for n in range(N_BRANCH)]
    yield
    merged = (gates[0] * proj[0] + gates[1] * proj[1]) + (gates[2] * proj[2] + gates[3] * proj[3])
    yield
    y_ref[...] = x_ref[...] + jnp.dot(_bf(merged), wo_ref[...], preferred_element_type=F32)
    yield


def _mixers_kernel(hq, hv, hg, hf, lb, og_h,
                   mqk, mv, mo, s1, s2, b0, b1, gain_m,
                   aq, kvp, kvc, qg, kg, sinks, bias,
                   dq, dk, dv, dqp, dkp, dvp, dz, conv_w, alog, dt, gain_d,
                   segm, eye16, seg64q, seg64k, seg128,
                   g0, g1, g2, g3, x_prev, wbr, wo,
                   y_ref, o_scr, st_h, c_m, m_m, st_d, *, tb, nt, total):
    s = pl.program_id(0)
    t_idx = lax.rem(jnp.minimum(s, total - 1), nt)

    @pl.when(s == 0)
    def _():
        o_scr[...] = jnp.zeros(o_scr.shape, BF16)

    at = lambda *refs: [r.at[0] for r in refs]
    o_a, o_b, o_c, o_d = (o_scr.at[:, n * BRANCH_W:(n + 1) * BRANCH_W] for n in range(N_BRANCH))
    _round_robin([
        _merge_body(o_scr, at(g0, g1, g2, g3), x_prev.at[0], wbr, wo, y_ref.at[0]),
        _dn_kernel(t_idx, *at(dq, dk, dv, dqp, dkp, dvp, dz, s1, s2), conv_w, alog, dt, gain_d,
                   segm, eye16, seg128, o_d, st_d, tb=tb),
        _mlstm_kernel(t_idx, *at(mqk, mv, mo, s1, s2), b0, b1, gain_m, segm, eye16,
                      o_b, c_m, m_m, tb=tb),
        _hgrn_kernel(t_idx, *at(hq, hv, hg, hf), lb, og_h, segm, o_a, st_h, tb=tb),
        _swa_kernel(t_idx, *at(aq, kvp, kvc), qg, kg, sinks, bias, seg64q, seg64k, o_c,
                    n_win=tb // WINDOW),
    ])


def _mixers(zb, zf, x3, lb, og_h, b0, b1, gain_m, qg, kg, sinks, bias, conv_w, alog, dt, gain_d,
            wbr, wo, batch, seq, tb):
    nt = seq // tb
    total = batch * nt
    cur = lambda s: jnp.minimum(s, total - 1)
    prv = lambda s: jnp.maximum(s - 1, 0)
    bt = lambda i: (i // nt, i % nt)
    blk = lambda c: pl.BlockSpec((1, tb, BRANCH_W), lambda s, c=c: (*bt(cur(s)), c // BRANCH_W))
    sm = lambda c: pl.BlockSpec((1, tb, LANES), lambda s, c=c: (*bt(cur(s)), c // LANES))
    full = lambda shape: pl.BlockSpec(shape, lambda s: (0,) * len(shape))
    kvw = 2 * LANES
    n_win = tb // WINDOW
    per = tb // CONV_TAIL
    kv_prev = pl.BlockSpec(
        (1, WINDOW, kvw),
        lambda s: (cur(s) // nt, jnp.maximum((cur(s) % nt) * n_win - 1, 0), ZB_AKV // kvw))
    dn_prev = lambda c: pl.BlockSpec(
        (1, CONV_TAIL, BRANCH_W),
        lambda s, c=c: (cur(s) // nt, jnp.maximum((cur(s) % nt) * per - 1, 0), c // BRANCH_W))
    gate = lambda k: pl.BlockSpec((1, tb, D_MODEL), lambda s, k=k: (*bt(prv(s)), ZB_GATE // D_MODEL + k))
    x_spec = pl.BlockSpec((1, tb, D_MODEL), lambda s: (*bt(prv(s)), 0))
    consts = _mixer_constants(tb)
    return pl.pallas_call(
        functools.partial(_mixers_kernel, tb=tb, nt=nt, total=total),
        grid=(total + 1,),
        in_specs=[
            blk(ZB_HQ), blk(ZB_HI), blk(ZB_HG), blk(ZF_HF), full((1, BRANCH_W)), full((1, BRANCH_W)),
            blk(ZB_MQK), blk(ZB_MV), blk(ZB_MO), sm(ZF_S1), sm(ZF_S2),
            full((1, LANES)), full((1, LANES)), full((1, BRANCH_W)),
            blk(ZB_AQ), kv_prev, pl.BlockSpec((1, tb, kvw), lambda s: (*bt(cur(s)), ZB_AKV // kvw)),
            full((1, BRANCH_W)), full((1, LANES)), full((AT_HEADS, LANES)),
            full((AT_HEADS, WINDOW, 2 * WINDOW)),
            blk(ZB_DQ), blk(ZB_DK), blk(ZB_DV), dn_prev(ZB_DQ), dn_prev(ZB_DK), dn_prev(ZB_DV),
            blk(ZB_DZ), full((CONV_K, 3 * BRANCH_W)), full((1, LANES)), full((1, LANES)),
            full((1, LANES)),
        ] + [full(c.shape) for c in consts] + [
            gate(0), gate(1), gate(2), gate(3), x_spec,
            full((N_BRANCH, BRANCH_W, D_MODEL)), full((D_MODEL, D_MODEL)),
        ],
        out_specs=x_spec,
        out_shape=jax.ShapeDtypeStruct((batch, seq, D_MODEL), F32),
        scratch_shapes=[pltpu.VMEM((tb, N_BRANCH * BRANCH_W), BF16),
                        pltpu.VMEM((N_HEADS, HEAD_W, HEAD_W), F32),
                        pltpu.VMEM((N_HEADS, LANES, 2 * HEAD_W), F32),
                        pltpu.VMEM((8, LANES), F32),
                        pltpu.VMEM((N_HEADS, DN_DK, HEAD_W), F32)],
        compiler_params=pltpu.CompilerParams(
            dimension_semantics=("arbitrary",),
            vmem_limit_bytes=VMEM_LIMIT),
        name="mixers",
    )(zb, zb, zb, zf, lb, og_h,
      zb, zb, zb, zf, zf, b0, b1, gain_m,
      zb, zb, zb, qg, kg, sinks, bias,
      zb, zb, zb, zb, zb, zb, zb, conv_w, alog, dt, gain_d, *consts,
      zb, zb, zb, zb, x3, wbr, wo)


def _mlp_kernel(x_ref, g_ref, wu_ref, wd_ref, y_ref):
    x = x_ref[...]
    ms = jnp.mean(x * x, axis=-1, keepdims=True)
    h = _bf((x * lax.rsqrt(ms + EPS)) * g_ref[...])
    up = jnp.dot(h, wu_ref[...], preferred_element_type=F32)
    act = jnp.square(jnp.maximum(up, 0.0))
    y_ref[...] = x + jnp.dot(_bf(act), wd_ref[...], preferred_element_type=F32)


def _mlp(x2, g, wu, wd, tm):
    n = x2.shape[0]
    return pl.pallas_call(
        _mlp_kernel,
        grid=(n // tm,),
        in_specs=[pl.BlockSpec((tm, D_MODEL), lambda i: (i, 0)),
                  pl.BlockSpec((1, D_MODEL), lambda i: (0, 0)),
                  pl.BlockSpec((D_MODEL, D_FF), lambda i: (0, 0)),
                  pl.BlockSpec((D_FF, D_MODEL), lambda i: (0, 0))],
        out_specs=pl.BlockSpec((tm, D_MODEL), lambda i: (i, 0)),
        out_shape=jax.ShapeDtypeStruct((n, D_MODEL), F32),
        compiler_params=pltpu.CompilerParams(
            dimension_semantics=("arbitrary",), vmem_limit_bytes=VMEM_LIMIT,
            allow_input_fusion=[False, False, True, True]),
        name="mlp",
    )(x2, g, wu, wd)


def _t5_bucket_table():
    n = np.arange(WINDOW)
    max_exact = N_BUCKETS // 2
    nf = np.maximum(n, max_exact).astype(np.float32)
    large = max_exact + (np.log(nf / np.float32(max_exact)) / np.float32(math.log(MAX_DISTANCE / max_exact))
                         * (N_BUCKETS - max_exact)).astype(np.int32)
    large = np.minimum(large, N_BUCKETS - 1)
    return np.where(n < max_exact, n, large)


def _swa_bias(rel_table):
    per_dist = rel_table.astype(F32)[_t5_bucket_table()].T
    n_heads = per_dist.shape[0]
    span = 3 * WINDOW
    pad_lo = jnp.full((n_heads, WINDOW - 1), NEG_BIG, F32)
    pad_hi = jnp.full((n_heads, span - 2 * WINDOW + 1), NEG_BIG, F32)
    v = jnp.concatenate([pad_lo, per_dist, pad_hi], axis=1)
    hank = jnp.tile(v, (1, WINDOW + 1))[:, :WINDOW * (span + 1)].reshape(n_heads, WINDOW, span + 1)
    return hank[:, :, :2 * WINDOW][:, :, ::-1]


def _lane_row(vals, offset):
    return jnp.zeros((1, LANES), F32).at[0, offset:offset + vals.shape[0]].set(vals.astype(F32))


def _layout_w_in(w):
    sizes = (512, 512, 512, 512, 256, 256, 512, 4, 4, 512, 512, 128, 128, 1536, 4, 4, 512, 4096)
    parts, start = [], 0
    for s in sizes:
        parts.append(_bf(w[:, start:start + s]))
        start += s
    (hq, hf, hi, hg, mq, mk, mv, mi, mf, mo, aq, ak, av, dqkv, db, da, dz, gate) = parts
    zpad = lambda n: jnp.zeros((w.shape[0], n), BF16)
    wb = jnp.concatenate([hq, hi, hg, mq, mk, mv, mo, aq, ak, av, zpad(256), dqkv, dz, gate], axis=1)
    wf = jnp.concatenate([hf, mi, db, zpad(LANES - 8), mf, da, zpad(LANES - 8)], axis=1)
    return wb, wf


def _tiles(n, seq):
    tiles = dict(
        inproj_rows=1024, inproj_cols=2560,
        mixer_rows=256,
        mlp_rows=512,
    )
    assert n % tiles["inproj_rows"] == 0 and ZB_W % tiles["inproj_cols"] == 0
    assert seq % tiles["mixer_rows"] == 0 and n % tiles["mlp_rows"] == 0
    return tiles


def kernel(x, norm_mix_g, w_in, hgrn_lb_table, hgrn_out_g, mlstm_if_bias, mlstm_out_g,
           attn_q_norm_g, attn_k_norm_g, attn_sinks, rel_bias_table, dn_conv_w, dn_a_log,
           dn_dt_bias, dn_out_g, w_branch, w_out, norm_mlp_g, w_up, w_down):
    batch, seq, d = x.shape
    depth = w_in.shape[0]
    n = batch * seq
    tiles = _tiles(n, seq)
    assert d == D_MODEL

    lb_p = jax.nn.softmax(hgrn_lb_table.astype(F32), axis=0)
    lower_bounds = jnp.cumsum(lb_p, axis=0) - lb_p[0]
    bias = _swa_bias(rel_bias_table)

    x2 = x.reshape(n, d)
    for l in range(depth):
        wb, wf = _layout_w_in(w_in[l])
        zb, zf = _inproj(x2, norm_mix_g[l].reshape(1, d), wb, wf, tiles["inproj_rows"], tiles["inproj_cols"])

        x2 = _mixers(
            zb.reshape(batch, seq, ZB_W), zf.reshape(batch, seq, ZF_W), x2.reshape(batch, seq, d),
            lower_bounds[l].reshape(1, -1), hgrn_out_g[l].reshape(1, -1).astype(F32),
            _lane_row(mlstm_if_bias[l, 0], 0), _lane_row(mlstm_if_bias[l, 1], 0),
            mlstm_out_g[l].reshape(1, -1).astype(F32),
            jnp.tile(attn_q_norm_g[l].astype(F32), AT_HEADS).reshape(1, -1),
            jnp.tile(attn_k_norm_g[l].astype(F32), 2).reshape(1, -1),
            jnp.broadcast_to(attn_sinks[l].astype(F32)[:, None], (AT_HEADS, LANES)), bias,
            dn_conv_w[l].astype(F32), _lane_row(dn_a_log[l], N_HEADS),
            _lane_row(dn_dt_bias[l], N_HEADS), dn_out_g[l].reshape(1, -1).astype(F32),
            _bf(w_branch[l]), _bf(w_out[l]), batch, seq, tiles["mixer_rows"]).reshape(n, d)
        x2 = _mlp(x2, norm_mlp_g[l].reshape(1, d), _bf(w_up[l]), _bf(w_down[l]), tiles["mlp_rows"])
    return x2.reshape(batch, seq, d)
```
